```python
import math
import jax, jax.numpy as jnp
from jax import lax
import numpy as np

D_MODEL = 2048
BATCH = 16
SEQ = 2048
DEPTH = 4

CTX_LEN = 256
GRID_W = 64
QBLK = 128
ROPE_BASE = 10000.0
EPS = 1e-6

DIFF_HEADS = 8
DIFF_HD = 64
DIFF_VD = 2 * DIFF_HD
DIFF_W = DIFF_HEADS * DIFF_VD
DIFF_SCALE = DIFF_HD ** -0.5

MLA_HEADS = 8
MLA_Q_RANK = 512
MLA_KV_RANK = 256
MLA_NOPE = 128
MLA_ROPE = 64
MLA_VD = 128
MLA_W = MLA_HEADS * MLA_VD
MLA_SCALE = (MLA_NOPE + MLA_ROPE) ** -0.5

ROPE_DIM = 64
AXIS_DIM = ROPE_DIM // 2

D_FF = 4 * D_MODEL
N_BRANCH = 2
N_MOD = 6

IN_SPLITS = (
    DIFF_HEADS * 2 * DIFF_HD,
    DIFF_HEADS * 2 * DIFF_HD,
    DIFF_HEADS * DIFF_VD,
    MLA_Q_RANK,
    MLA_KV_RANK,
    MLA_ROPE,
    N_BRANCH * D_MODEL,
)
IN_W = sum(IN_SPLITS)

kernel_name = "hybrid_diffattn_mla_dit_trunk"


def rmsnorm(x, g):
    xf = x.astype(jnp.float32)
    y = xf * lax.rsqrt(jnp.mean(xf * xf, axis=-1, keepdims=True) + EPS)
    return (y * g.astype(jnp.float32)).astype(x.dtype)


def modulate(h, shift, scale):
    return h * (1 + scale) + shift


def axial_tables(n_tokens):
    rows = n_tokens // GRID_W
    row, col = jnp.meshgrid(jnp.arange(rows), jnp.arange(GRID_W), indexing="ij")
    row = row.reshape(-1).astype(jnp.float32)
    col = col.reshape(-1).astype(jnp.float32)
    freqs = ROPE_BASE ** (-jnp.arange(0, AXIS_DIM, 2, dtype=jnp.float32) / AXIS_DIM)
    ang_r = row[:, None] * freqs
    ang_c = col[:, None] * freqs
    ang = jnp.concatenate([ang_r, ang_r, ang_c, ang_c], axis=-1)
    return jnp.cos(ang), jnp.sin(ang)


def rotate_half_axial(x):
    h = AXIS_DIM // 2
    return jnp.concatenate([-x[..., h:AXIS_DIM], x[..., :h],
                            -x[..., AXIS_DIM + h:], x[..., AXIS_DIM:AXIS_DIM + h]], axis=-1)


def apply_rope(x, rope):
    cos, sin = rope
    xf = x.astype(jnp.float32)
    out = xf * cos[None, :, None, :] + rotate_half_axial(xf) * sin[None, :, None, :]
    return out.astype(x.dtype)


def sweep_query_blocks(fn, qs):
    b, t = qs[0].shape[:2]
    nb = t // QBLK
    blocks = tuple(jnp.moveaxis(q.reshape(b, nb, QBLK, *q.shape[2:]), 1, 0) for q in qs)
    out = lax.map(lambda blk: fn(*blk), blocks)
    out = jnp.moveaxis(out, 0, 1)
    return out.reshape(b, t, *out.shape[3:])


def diff_block(q1, q2, k1, k2, v, lam):
    s1 = jnp.einsum("bqhd,bkhd->bhqk", q1, k1).astype(jnp.float32) * DIFF_SCALE
    s2 = jnp.einsum("bqhd,bkhd->bhqk", q2, k2).astype(jnp.float32) * DIFF_SCALE
    a = jax.nn.softmax(s1, axis=-1) - lam * jax.nn.softmax(s2, axis=-1)
    return jnp.einsum("bhqk,bkhd->bqhd", a.astype(v.dtype), v)


def mla_block(q, k, v):
    s = jnp.einsum("bqhd,bkhd->bhqk", q, k).astype(jnp.float32) * MLA_SCALE
    p = jax.nn.softmax(s, axis=-1)
    return jnp.einsum("bhqk,bkhd->bqhd", p.astype(v.dtype), v)


def mixer_projections(h, w_in, b_gate, q_a_norm, w_uq, kv_a_norm, w_ukv, rope):
    b, t, _ = h.shape
    offsets = [int(o) for o in np.cumsum(IN_SPLITS)[:-1]]
    qd, kd, vd, cq, ckv, kr, gpre = jnp.split(h @ w_in, offsets, axis=-1)
    qd = qd.reshape(b, t, DIFF_HEADS, 2, DIFF_HD)
    kd = kd.reshape(b, t, DIFF_HEADS, 2, DIFF_HD)
    q1, q2 = qd[..., 0, :], qd[..., 1, :]
    k1, k2 = kd[..., 0, :], kd[..., 1, :]
    v_d = vd.reshape(b, t, DIFF_HEADS, DIFF_VD)
    qm = (rmsnorm(cq, q_a_norm) @ w_uq).reshape(b, t, MLA_HEADS, MLA_NOPE + MLA_ROPE)
    q_nope, q_pe = qm[..., :MLA_NOPE], qm[..., MLA_NOPE:]
    kv = (rmsnorm(ckv, kv_a_norm) @ w_ukv).reshape(b, t, MLA_HEADS, MLA_NOPE + MLA_VD)
    k_nope, v_m = kv[..., :MLA_NOPE], kv[..., MLA_NOPE:]
    k_pe = kr.reshape(b, t, 1, MLA_ROPE)
    if rope is not None:
        q1, q2, k1, k2 = (apply_rope(a, rope) for a in (q1, q2, k1, k2))
        q_pe = apply_rope(q_pe, rope)
        k_pe = apply_rope(k_pe, rope)
    q_m = jnp.concatenate([q_nope, q_pe], axis=-1)
    k_m = jnp.concatenate([k_nope, jnp.broadcast_to(k_pe, (b, t, MLA_HEADS, MLA_ROPE))], axis=-1)
    gates = jax.nn.sigmoid((gpre + b_gate).astype(jnp.float32)).astype(h.dtype)
    return (q1, q2, q_m), (k1, k2, v_d, k_m, v_m), gates


def mix(queries, keys, gates, lam, lam_init, diff_subln, w_o_diff, w_o_mla, w_out):
    q1, q2, q_m = queries
    k1, k2, v_d, k_m, v_m = keys
    b, t = q1.shape[:2]
    y_d = sweep_query_blocks(lambda a, c: diff_block(a, c, k1, k2, v_d, lam), (q1, q2))
    y_d = rmsnorm(y_d, diff_subln) * (1.0 - lam_init)
    y_m = sweep_query_blocks(lambda a: mla_block(a, k_m, v_m), (q_m,))
    y_d = y_d.reshape(b, t, DIFF_W) @ w_o_diff
    y_m = y_m.reshape(b, t, MLA_W) @ w_o_mla
    merged = gates[..., :D_MODEL] * y_d + gates[..., D_MODEL:] * y_m
    return merged @ w_out


def sq_relu_mlp(h, w1, w2):
    u = jax.nn.relu(h @ w1)
    return (u * u) @ w2


def setup_inputs(seed: int = 0) -> dict:
    key = jax.random.key(seed)
    ks = jax.random.split(key, 24)

    def nrm(k, shape, scale):
        return jax.random.normal(k, shape, jnp.float32) * scale

    def gain(k, shape):
        return 1.0 + 0.02 * jax.random.normal(k, shape, jnp.float32)

    L, D = DEPTH, D_MODEL
    return {
        "x": nrm(ks[0], (BATCH, SEQ, D), 1.0),
        "c": nrm(ks[1], (BATCH, D), 1.0),
        "ctx": nrm(ks[2], (BATCH, CTX_LEN, D), 1.0),
        "c_ctx": nrm(ks[3], (D,), 0.5),
        "w_ada": nrm(ks[4], (L, D, N_MOD * D), 0.5 * D ** -0.5),
        "b_ada": nrm(ks[5], (L, N_MOD * D), 0.01),
        "norm1_g": gain(ks[6], (L, D)),
        "norm2_g": gain(ks[7], (L, D)),
        "w_in": nrm(ks[8], (L, D, IN_W), D ** -0.5),
        "b_gate": nrm(ks[9], (L, N_BRANCH * D), 0.01),
        "q_a_norm": gain(ks[10], (L, MLA_Q_RANK)),
        "w_uq": nrm(ks[11], (L, MLA_Q_RANK, MLA_HEADS * (MLA_NOPE + MLA_ROPE)), MLA_Q_RANK ** -0.5),
        "kv_a_norm": gain(ks[12], (L, MLA_KV_RANK)),
        "w_ukv": nrm(ks[13], (L, MLA_KV_RANK, MLA_HEADS * (MLA_NOPE + MLA_VD)), MLA_KV_RANK ** -0.5),
        "diff_lambda": nrm(ks[14], (L, 4, DIFF_HD), 0.1),
        "diff_subln": gain(ks[15], (L, DIFF_VD)),
        "w_o_diff": nrm(ks[16], (L, DIFF_W, D), DIFF_W ** -0.5),
        "w_o_mla": nrm(ks[17], (L, MLA_W, D), MLA_W ** -0.5),
        "w_out": nrm(ks[18], (L, D, D), D ** -0.5),
        "w_mlp1": nrm(ks[19], (L, D, D_FF), D ** -0.5),
        "w_mlp2": nrm(ks[20], (L, D_FF, D), D_FF ** -0.5),
        "final_norm_g": gain(ks[21], (D,)),
    }


def reference(x, c, ctx, c_ctx, w_ada, b_ada, norm1_g, norm2_g, w_in, b_gate, q_a_norm, w_uq,
              kv_a_norm, w_ukv, diff_lambda, diff_subln, w_o_diff, w_o_mla, w_out, w_mlp1,
              w_mlp2, final_norm_g):
    seq = x.shape[1]
    rope = axial_tables(seq)
    xc = ctx
    silu_c = jax.nn.silu(c)
    silu_cc = jax.nn.silu(c_ctx)
    for l in range(DEPTH):
        mod_x = (silu_c @ w_ada[l] + b_ada[l])[:, None, :]
        mod_c = silu_cc @ w_ada[l] + b_ada[l]
        sh1, sc1, g1, sh2, sc2, g2 = jnp.split(mod_x, N_MOD, axis=-1)
        csh1, csc1, cg1, csh2, csc2, cg2 = jnp.split(mod_c, N_MOD, axis=-1)
        lam_init = 0.8 - 0.6 * math.exp(-0.3 * l)
        dl = diff_lambda[l].astype(jnp.float32)
        lam = jnp.exp(jnp.sum(dl[0] * dl[1])) - jnp.exp(jnp.sum(dl[2] * dl[3])) + lam_init
        proj_w = (w_in[l], b_gate[l], q_a_norm[l], w_uq[l], kv_a_norm[l], w_ukv[l])
        mix_w = (lam, lam_init, diff_subln[l], w_o_diff[l], w_o_mla[l], w_out[l])

        hx = modulate(rmsnorm(x, norm1_g[l]), sh1, sc1)
        hc = modulate(rmsnorm(xc, norm1_g[l]), csh1, csc1)
        qx, kx, gx = mixer_projections(hx, *proj_w, rope)
        qc, kc, gc = mixer_projections(hc, *proj_w, None)
        keys_lat = tuple(jnp.concatenate([a, b], axis=1) for a, b in zip(kc, kx))
        x = x + g1 * mix(qx, keys_lat, gx, *mix_w)
        x = x + g2 * sq_relu_mlp(modulate(rmsnorm(x, norm2_g[l]), sh2, sc2), w_mlp1[l], w_mlp2[l])
        if l < DEPTH - 1:
            xc = xc + cg1 * mix(qc, kc, gc, *mix_w)
            xc = xc + cg2 * sq_relu_mlp(modulate(rmsnorm(xc, norm2_g[l]), csh2, csc2),
                                        w_mlp1[l], w_mlp2[l])
    return rmsnorm(x, final_norm_g)
```

```python
import functools
import math

import jax
import jax.numpy as jnp
from jax import lax
from jax.experimental import pallas as pl
from jax.experimental.pallas import tpu as pltpu

D_MODEL = 2048
BATCH = 16
SEQ = 2048
DEPTH = 4
CTX_LEN = 256
GRID_W = 64
ROPE_BASE = 10000.0
EPS = 1e-6

DIFF_HEADS = 8
DIFF_HD = 64
DIFF_VD = 2 * DIFF_HD
DIFF_W = DIFF_HEADS * DIFF_VD
DIFF_SCALE = DIFF_HD ** -0.5

MLA_HEADS = 8
MLA_Q_RANK = 512
MLA_KV_RANK = 256
MLA_NOPE = 128
MLA_ROPE = 64
MLA_VD = 128
MLA_W = MLA_HEADS * MLA_VD
MLA_SCALE = (MLA_NOPE + MLA_ROPE) ** -0.5
MLA_QK_PAD = 256

ROPE_DIM = 64
AXIS_DIM = ROPE_DIM // 2
D_FF = 4 * D_MODEL
N_MOD = 6

M_CTX = BATCH * CTX_LEN
M_LAT = BATCH * SEQ
M_ALL = M_CTX + M_LAT
MOD_ROWS = 24

LANES = 128
LAT_W = 1024

F32 = jnp.float32
BF16 = jnp.bfloat16

V7X_VMEM_BYTES = 64 * 1024 * 1024
VMEM_LIMIT = V7X_VMEM_BYTES - 8 * 1024 * 1024


def _params(sem):
    return pltpu.CompilerParams(dimension_semantics=sem, vmem_limit_bytes=VMEM_LIMIT)


def _rmsnorm(x, g):
    return x * lax.rsqrt(jnp.mean(x * x, axis=-1, keepdims=True) + EPS) * g


def _norm_modulate(x, g, shift, scale):
    return _rmsnorm(x, g) * (1.0 + scale) + shift


def _rope128(x, cos, sin_signed):
    lane = lax.broadcasted_iota(jnp.int32, x.shape, 1)
    first = (lane & 16) == 0
    rot = jnp.where(first, pltpu.roll(x, LANES - 16, 1), pltpu.roll(x, 16, 1))
    return x * cos + rot * sin_signed


def _mod_row(i, tiles_ctx, tiles_per_batch):
    return jnp.where(i < tiles_ctx, 0, 1 + (i - tiles_ctx) // tiles_per_batch)


def _ada_kernel(c_ref, w_ref, b_ref, o_ref):
    c = c_ref[...]
    s = (c * jax.nn.sigmoid(c)).astype(BF16)
    o_ref[...] = jnp.dot(s, w_ref[...].astype(BF16), preferred_element_type=F32) + b_ref[...]


def _ada_table(cond, w_ada, b_ada):
    tn = 1024
    n = N_MOD * D_MODEL
    return pl.pallas_call(
        _ada_kernel,
        grid=(DEPTH, n // tn),
        in_specs=[
            pl.BlockSpec((MOD_ROWS, D_MODEL), lambda l, j: (0, 0)),
            pl.BlockSpec((None, D_MODEL, tn), lambda l, j: (l, 0, j)),
            pl.BlockSpec((None, 1, tn), lambda l, j: (l, 0, j)),
        ],
        out_specs=pl.BlockSpec((None, MOD_ROWS, tn), lambda l, j: (l, 0, j)),
        out_shape=jax.ShapeDtypeStruct((DEPTH, MOD_ROWS, n), F32),
        compiler_params=_params(("parallel", "parallel")),
        name="ada_table",
    )(cond, w_ada, b_ada.reshape(DEPTH, 1, n))


def _embed_kernel(ctx_ref, x_ref, g_ref, mod_ref, xo_ref, h_ref, *, tiles_ctx):
    i = pl.program_id(0)
    mod = mod_ref[...]
    shift = mod[:, 0:D_MODEL]
    scale = mod[:, D_MODEL:2 * D_MODEL]

    def emit(src):
        xo_ref[...] = src
        h_ref[...] = _norm_modulate(src, g_ref[...], shift, scale).astype(BF16)

    @pl.when(i < tiles_ctx)
    def _():
        emit(ctx_ref[...])

    @pl.when(i >= tiles_ctx)
    def _():
        emit(x_ref[...])


def _embed(ctx2d, x2d, g, mods):
    tm = 512
    tiles_ctx = M_CTX // tm
    tpb = SEQ // tm
    return pl.pallas_call(
        functools.partial(_embed_kernel, tiles_ctx=tiles_ctx),
        grid=(M_ALL // tm,),
        in_specs=[
            pl.BlockSpec((tm, D_MODEL), lambda i: (jnp.minimum(i, tiles_ctx - 1), 0)),
            pl.BlockSpec((tm, D_MODEL), lambda i: (jnp.maximum(i - tiles_ctx, 0), 0)),
            pl.BlockSpec((1, D_MODEL), lambda i: (0, 0)),
            pl.BlockSpec((None, 1, N_MOD * D_MODEL), lambda i: (_mod_row(i, tiles_ctx, tpb), 0, 0)),
        ],
        out_specs=[
            pl.BlockSpec((tm, D_MODEL), lambda i: (i, 0)),
            pl.BlockSpec((tm, D_MODEL), lambda i: (i, 0)),
        ],
        out_shape=[
            jax.ShapeDtypeStruct((M_ALL, D_MODEL), F32),
            jax.ShapeDtypeStruct((M_ALL, D_MODEL), BF16),
        ],
        compiler_params=_params(("parallel",)),
        name="embed",
    )(ctx2d, x2d, g, mods)


def _qkv_kernel(h_ref, w_ref, cos_ref, sin_ref, o_ref, *, tiles_ctx, tn):
    i = pl.program_id(0)
    j = pl.program_id(1)
    acc = jnp.dot(h_ref[...], w_ref[...], preferred_element_type=F32)
    rotate = jnp.logical_and(i >= tiles_ctx, j < 2)

    @pl.when(rotate)
    def _():
        cos = cos_ref[...]
        sin = sin_ref[...]
        for c in range(tn // LANES):
            sl = slice(c * LANES, (c + 1) * LANES)
            o_ref[:, sl] = _rope128(acc[:, sl], cos, sin).astype(BF16)

    @pl.when(jnp.logical_not(rotate))
    def _():
        o_ref[...] = acc.astype(BF16)


def _qkv_proj(h, w_qkv, cos2, sin2):
    tm, tn = 1024, 1024
    tiles_ctx = M_CTX // tm
    tpb = SEQ // tm
    pos = lambda i, j: (jnp.maximum(i - tiles_ctx, 0) % tpb, 0)
    return pl.pallas_call(
        functools.partial(_qkv_kernel, tiles_ctx=tiles_ctx, tn=tn),
        grid=(M_ALL // tm, 3 * DIFF_W // tn),
        in_specs=[
            pl.BlockSpec((tm, D_MODEL), lambda i, j: (i, 0)),
            pl.BlockSpec((D_MODEL, tn), lambda i, j: (0, j)),
            pl.BlockSpec((tm, LANES), pos),
            pl.BlockSpec((tm, LANES), pos),
        ],
        out_specs=pl.BlockSpec((tm, tn), lambda i, j: (i, j)),
        out_shape=jax.ShapeDtypeStruct((M_ALL, 3 * DIFF_W), BF16),
        compiler_params=_params(("parallel", "parallel")),
        name="qkv_proj",
    )(h, w_qkv, cos2, sin2)


def _lat_kernel(h_ref, wlat_ref, qg_ref, kvg_ref, wuq_ref, wukv_ref, cos_ref, sin_ref,
                qm_ref, km_ref, vm_ref, *, tiles_ctx):
    i = pl.program_id(0)
    is_lat = i >= tiles_ctx
    acc = jnp.dot(h_ref[...], wlat_ref[...], preferred_element_type=F32)
    cq = acc[:, 0:MLA_Q_RANK]
    ckv = acc[:, MLA_Q_RANK:MLA_Q_RANK + MLA_KV_RANK]
    kr = acc[:, MLA_Q_RANK + MLA_KV_RANK:MLA_Q_RANK + MLA_KV_RANK + LANES]
    cos = cos_ref[...]
    sin = sin_ref[...]

    cqn = _rmsnorm(cq, qg_ref[...]).astype(BF16)
    qm = jnp.dot(cqn, wuq_ref[...], preferred_element_type=F32) * MLA_SCALE
    ckvn = _rmsnorm(ckv, kvg_ref[...]).astype(BF16)
    kv = jnp.dot(ckvn, wukv_ref[...], preferred_element_type=F32)
    kpe = jnp.where(is_lat, _rope128(kr, cos, sin), kr).astype(BF16)

    for hd in range(MLA_HEADS):
        q0 = hd * MLA_QK_PAD
        qm_ref[:, q0:q0 + LANES] = qm[:, q0:q0 + LANES].astype(BF16)
        pe = qm[:, q0 + LANES:q0 + 2 * LANES]
        qm_ref[:, q0 + LANES:q0 + 2 * LANES] = jnp.where(is_lat, _rope128(pe, cos, sin), pe).astype(BF16)
        km_ref[:, q0:q0 + LANES] = kv[:, hd * LANES:(hd + 1) * LANES].astype(BF16)
        km_ref[:, q0 + LANES:q0 + 2 * LANES] = kpe
    vm_ref[...] = kv[:, MLA_W:2 * MLA_W].astype(BF16)


def _lat_proj(h, w_lat, qg, kvg, w_uq, w_ukv, cos_pe, sin_pe):
    tm = 512
    tiles_ctx = M_CTX // tm
    tpb = SEQ // tm
    pos = lambda i: (jnp.maximum(i - tiles_ctx, 0) % tpb, 0)
    const = lambda i: (0, 0)
    qk_w = MLA_HEADS * MLA_QK_PAD
    return pl.pallas_call(
        functools.partial(_lat_kernel, tiles_ctx=tiles_ctx),
        grid=(M_ALL // tm,),
        in_specs=[
            pl.BlockSpec((tm, D_MODEL), lambda i: (i, 0)),
            pl.BlockSpec((D_MODEL, LAT_W), const),
            pl.BlockSpec((1, MLA_Q_RANK), const),
            pl.BlockSpec((1, MLA_KV_RANK), const),
            pl.BlockSpec((MLA_Q_RANK, qk_w), const),
            pl.BlockSpec((MLA_KV_RANK, 2 * MLA_W), const),
            pl.BlockSpec((tm, LANES), pos),
            pl.BlockSpec((tm, LANES), pos),
        ],
        out_specs=[
            pl.BlockSpec((tm, qk_w), lambda i: (i, 0)),
            pl.BlockSpec((tm, qk_w), lambda i: (i, 0)),
            pl.BlockSpec((tm, MLA_W), lambda i: (i, 0)),
        ],
        out_shape=[
            jax.ShapeDtypeStruct((M_ALL, qk_w), BF16),
            jax.ShapeDtypeStruct((M_ALL, qk_w), BF16),
            jax.ShapeDtypeStruct((M_ALL, MLA_W), BF16),
        ],
        compiler_params=_params(("parallel",)),
        name="lat_proj",
    )(h, w_lat, qg, kvg, w_uq, w_ukv, cos_pe, sin_pe)


def _gate_kernel(h_ref, w_ref, b_ref, o_ref):
    acc = jnp.dot(h_ref[...], w_ref[...], preferred_element_type=F32)
    o_ref[...] = jax.nn.sigmoid(acc + b_ref[...]).astype(BF16)


def _gate_proj(h, w_gate, b_gate):
    tm, tn = 1024, 1024
    n = 2 * D_MODEL
    return pl.pallas_call(
        _gate_kernel,
        grid=(M_ALL // tm, n // tn),
        in_specs=[
            pl.BlockSpec((tm, D_MODEL), lambda i, j: (i, 0)),
            pl.BlockSpec((D_MODEL, tn), lambda i, j: (0, j)),
            pl.BlockSpec((1, tn), lambda i, j: (0, j)),
        ],
        out_specs=pl.BlockSpec((tm, tn), lambda i, j: (i, j)),
        out_shape=jax.ShapeDtypeStruct((M_ALL, n), BF16),
        compiler_params=_params(("parallel", "parallel")),
        name="gate_proj",
    )(h, w_gate, b_gate)


def _softmax_pv(q, k_refs, v_refs):
    dims = (((1,), (1,)), ((), ()))
    s = [lax.dot_general(q, k[...], dims, preferred_element_type=F32) for k in k_refs]
    m = s[0].max(axis=1, keepdims=True)
    for t in s[1:]:
        m = jnp.maximum(m, t.max(axis=1, keepdims=True))
    p = [jnp.exp(t - m) for t in s]
    l = p[0].sum(axis=1, keepdims=True)
    for t in p[1:]:
        l = l + t.sum(axis=1, keepdims=True)
    o = jnp.dot(p[0].astype(BF16), v_refs[0][...], preferred_element_type=F32)
    for t, v in zip(p[1:], v_refs[1:]):
        o = o + jnp.dot(t.astype(BF16), v[...], preferred_element_type=F32)
    return o / l


def _diff_attn_kernel(*refs, n_kv, lam_init):
    dl_ref, g_ref, q_ref = refs[0:3]
    k_refs = refs[3:3 + n_kv]
    v_refs = refs[3 + n_kv:3 + 2 * n_kv]
    o_ref = refs[3 + 2 * n_kv]
    tq = q_ref.shape[0]
    dl = dl_ref[...]
    lam = (jnp.exp(jnp.sum(dl[0:1, :] * dl[1:2, :], axis=1, keepdims=True))
           - jnp.exp(jnp.sum(dl[2:3, :] * dl[3:4, :], axis=1, keepdims=True)) + lam_init)
    q = q_ref[...].astype(F32)
    lane = lax.broadcasted_iota(jnp.int32, q.shape, 1)
    q2 = jnp.concatenate([jnp.where(lane < DIFF_HD, q, 0.0), jnp.where(lane >= DIFF_HD, q, 0.0)],
                         axis=0).astype(BF16)
    o = _softmax_pv(q2, k_refs, v_refs)
    y = o[0:tq] - lam * o[tq:2 * tq]
    o_ref[...] = (_rmsnorm(y, g_ref[...]) * (1.0 - lam_init)).astype(BF16)


def _mla_attn_kernel(*refs, n_kv):
    q_ref = refs[0]
    k_refs = refs[1:1 + n_kv]
    v_refs = refs[1 + n_kv:1 + 2 * n_kv]
    o_ref = refs[1 + 2 * n_kv]
    o_ref[...] = _softmax_pv(q_ref[...], k_refs, v_refs).astype(BF16)


def _attn_specs(tq, dk, q_col0, k_col0, v_col0, latent, out_row0):
    if latent:
        row0 = M_CTX // tq
        orow0 = out_row0 // tq
        per_b = SEQ // tq
        q_map = lambda b, h, i: (row0 + b * per_b + i, q_col0 + h)
        o_map = lambda b, h, i: (orow0 + b * per_b + i, h)
    else:
        q_map = lambda b, h, i: (b, q_col0 + h)
        o_map = lambda b, h, i: (b, h)
    q_spec = pl.BlockSpec((tq, dk), q_map)
    kc = pl.BlockSpec((CTX_LEN, dk), lambda b, h, i: (b, k_col0 + h))
    vc = pl.BlockSpec((CTX_LEN, LANES), lambda b, h, i: (b, v_col0 + h))
    k_specs, v_specs = [kc], [vc]
    if latent:
        lat0 = M_CTX // SEQ
        k_specs.append(pl.BlockSpec((SEQ, dk), lambda b, h, i: (lat0 + b, k_col0 + h)))
        v_specs.append(pl.BlockSpec((SEQ, LANES), lambda b, h, i: (lat0 + b, v_col0 + h)))
    o_spec = pl.BlockSpec((tq, LANES), o_map)
    return q_spec, k_specs, v_specs, o_spec


def _diff_attn(qkv, dl, g, lam_init, out_row0=M_CTX, prev=None):
    latent = prev is None
    tq = 256 if latent else CTX_LEN
    nq = SEQ // tq if latent else 1
    q_spec, k_specs, v_specs, o_spec = _attn_specs(tq, LANES, 0, DIFF_HEADS, 2 * DIFF_HEADS, latent, out_row0)
    n_kv = len(k_specs)
    const = lambda b, h, i: (0, 0)
    in_specs = [pl.BlockSpec((4, DIFF_HD), const), pl.BlockSpec((1, DIFF_VD), const), q_spec] + k_specs + v_specs
    args = [dl, g, qkv] + [qkv] * (2 * n_kv)
    aliases = {}
    if not latent:
        in_specs.append(pl.BlockSpec(memory_space=pl.ANY))
        args.append(prev)
        aliases = {len(args) - 1: 0}
    kern = functools.partial(_diff_attn_kernel, n_kv=n_kv, lam_init=lam_init)
    if not latent:
        kern = functools.partial(_drop_last_input, kern, n_in=len(args))
    return pl.pallas_call(
        kern,
        grid=(BATCH, DIFF_HEADS, nq),
        in_specs=in_specs,
        out_specs=o_spec,
        out_shape=jax.ShapeDtypeStruct((out_row0 + M_LAT, DIFF_W), BF16),
        input_output_aliases=aliases,
        compiler_params=_params(("parallel", "parallel", "parallel")),
        name="diff_attn_lat" if latent else "diff_attn_ctx",
    )(*args)


def _drop_last_input(kern, *refs, n_in):
    kern(*refs[:n_in - 1], *refs[n_in:])


def _mla_attn(qm, km, vm, out_row0=M_CTX, prev=None):
    latent = prev is None
    tq = 256 if latent else CTX_LEN
    nq = SEQ // tq if latent else 1
    q_spec, k_specs, v_specs, o_spec = _attn_specs(tq, MLA_QK_PAD, 0, 0, 0, latent, out_row0)
    n_kv = len(k_specs)
    in_specs = [q_spec] + k_specs + v_specs
    args = [qm] + [km] * n_kv + [vm] * n_kv
    aliases = {}
    if not latent:
        in_specs.append(pl.BlockSpec(memory_space=pl.ANY))
        args.append(prev)
        aliases = {len(args) - 1: 0}
    kern = functools.partial(_mla_attn_kernel, n_kv=n_kv)
    if not latent:
        kern = functools.partial(_drop_last_input, kern, n_in=len(args))
    return pl.pallas_call(
        kern,
        grid=(BATCH, MLA_HEADS, nq),
        in_specs=in_specs,
        out_specs=o_spec,
        out_shape=jax.ShapeDtypeStruct((out_row0 + M_LAT, MLA_W), BF16),
        input_output_aliases=aliases,
        compiler_params=_params(("parallel", "parallel", "parallel")),
        name="mla_attn_lat" if latent else "mla_attn_ctx",
    )(*args)


def _merge_kernel(yd_ref, ym_ref, wd_ref, wm_ref, gd_ref, gm_ref, o_ref):
    yd = jnp.dot(yd_ref[...], wd_ref[...], preferred_element_type=F32)
    ym = jnp.dot(ym_ref[...], wm_ref[...], preferred_element_type=F32)
    o_ref[...] = (gd_ref[...].astype(F32) * yd + gm_ref[...].astype(F32) * ym).astype(BF16)


def _merge(yd, ym, w_od, w_om, gates, row0):
    tm, tn = 1024, 1024
    t0 = row0 // tm
    nj = D_MODEL // tn
    return pl.pallas_call(
        _merge_kernel,
        grid=((M_ALL - row0) // tm, nj),
        in_specs=[
            pl.BlockSpec((tm, DIFF_W), lambda i, j: (i, 0)),
            pl.BlockSpec((tm, MLA_W), lambda i, j: (i, 0)),
            pl.BlockSpec((DIFF_W, tn), lambda i, j: (0, j)),
            pl.BlockSpec((MLA_W, tn), lambda i, j: (0, j)),
            pl.BlockSpec((tm, tn), lambda i, j: (t0 + i, j)),
            pl.BlockSpec((tm, tn), lambda i, j: (t0 + i, nj + j)),
        ],
        out_specs=pl.BlockSpec((tm, tn), lambda i, j: (i, j)),
        out_shape=jax.ShapeDtypeStruct((M_ALL - row0, D_MODEL), BF16),
        compiler_params=_params(("parallel", "parallel")),
        name="merge",
    )(yd, ym, w_od, w_om, gates, gates)


def _out_kernel(a_ref, w_ref, x_ref, g_ref, mod_ref, xo_ref, h_ref):
    mod = mod_ref[...]
    gate1 = mod[:, 2 * D_MODEL:3 * D_MODEL]
    shift2 = mod[:, 3 * D_MODEL:4 * D_MODEL]
    scale2 = mod[:, 4 * D_MODEL:5 * D_MODEL]
    x1 = x_ref[...] + gate1 * jnp.dot(a_ref[...], w_ref[...], preferred_element_type=F32)
    xo_ref[...] = x1
    h_ref[...] = _norm_modulate(x1, g_ref[...], shift2, scale2).astype(BF16)


def _out_proj(merged, w_out, x, g2, mods, row0):
    tm = 512
    t0 = row0 // tm
    tiles_ctx = M_CTX // tm
    tpb = SEQ // tm
    rows = lambda i: (i, 0)
    const = lambda i: (0, 0)
    return pl.pallas_call(
        _out_kernel,
        grid=((M_ALL - row0) // tm,),
        in_specs=[
            pl.BlockSpec((tm, D_MODEL), rows),
            pl.BlockSpec((D_MODEL, D_MODEL), const),
            pl.BlockSpec((tm, D_MODEL), lambda i: (t0 + i, 0)),
            pl.BlockSpec((1, D_MODEL), const),
            pl.BlockSpec((None, 1, N_MOD * D_MODEL), lambda i: (_mod_row(t0 + i, tiles_ctx, tpb), 0, 0)),
        ],
        out_specs=[pl.BlockSpec((tm, D_MODEL), rows), pl.BlockSpec((tm, D_MODEL), rows)],
        out_shape=[
            jax.ShapeDtypeStruct((M_ALL - row0, D_MODEL), F32),
            jax.ShapeDtypeStruct((M_ALL - row0, D_MODEL), BF16),
        ],
        compiler_params=_params(("parallel",)),
        name="out_proj",
    )(merged, w_out, x, g2, mods)


def _mlp_kernel(h_ref, w1_ref, w2_ref, x_ref, mod_ref, gn_ref, modn_ref, *out_refs, final):
    acc_ref = out_refs[-1]
    f = pl.program_id(1)

    @pl.when(f == 0)
    def _():
        acc_ref[...] = jnp.zeros_like(acc_ref)

    u = jnp.maximum(jnp.dot(h_ref[...], w1_ref[...], preferred_element_type=F32), 0.0)
    acc_ref[...] += jnp.dot((u * u).astype(BF16), w2_ref[...], preferred_element_type=F32)

    @pl.when(f == pl.num_programs(1) - 1)
    def _():
        gate2 = mod_ref[...][:, 5 * D_MODEL:6 * D_MODEL]
        x2 = x_ref[...] + gate2 * acc_ref[...]
        if final:
            out_refs[0][...] = _rmsnorm(x2, gn_ref[...])
        else:
            modn = modn_ref[...]
            out_refs[0][...] = x2
            out_refs[1][...] = _norm_modulate(
                x2, gn_ref[...], modn[:, 0:D_MODEL], modn[:, D_MODEL:2 * D_MODEL]).astype(BF16)


def _mlp(h2, w1, w2, x1, mods, g_next, mods_next, row0, final):
    tm, tf = 512, 512
    t0 = row0 // tm
    tiles_ctx = M_CTX // tm
    tpb = SEQ // tm
    rows = lambda i, f: (i, 0)
    mod_map = lambda i, f: (_mod_row(t0 + i, tiles_ctx, tpb), 0, 0)
    if final:
        out_specs = [pl.BlockSpec((tm, D_MODEL), rows)]
        out_shape = [jax.ShapeDtypeStruct((M_ALL - row0, D_MODEL), F32)]
    else:
        out_specs = [pl.BlockSpec((tm, D_MODEL), rows), pl.BlockSpec((tm, D_MODEL), rows)]
        out_shape = [jax.ShapeDtypeStruct((M_ALL - row0, D_MODEL), F32),
                     jax.ShapeDtypeStruct((M_ALL - row0, D_MODEL), BF16)]
    return pl.pallas_call(
        functools.partial(_mlp_kernel, final=final),
        grid=((M_ALL - row0) // tm, D_FF // tf),
        in_specs=[
            pl.BlockSpec((tm, D_MODEL), rows),
            pl.BlockSpec((D_MODEL, tf), lambda i, f: (0, f)),
            pl.BlockSpec((tf, D_MODEL), lambda i, f: (f, 0)),
            pl.BlockSpec((tm, D_MODEL), rows),
            pl.BlockSpec((None, 1, N_MOD * D_MODEL), mod_map),
            pl.BlockSpec((1, D_MODEL), lambda i, f: (0, 0)),
            pl.BlockSpec((None, 1, N_MOD * D_MODEL), mod_map),
        ],
        out_specs=out_specs,
        out_shape=out_shape,
        scratch_shapes=[pltpu.VMEM((tm, D_MODEL), F32)],
        compiler_params=_params(("parallel", "arbitrary")),
        name="mlp_final" if final else "mlp",
    )(h2, w1, w2, x1, mods, g_next, mods_next)


def _rope_tables():
    rows = SEQ // GRID_W
    row, col = jnp.meshgrid(jnp.arange(rows), jnp.arange(GRID_W), indexing="ij")
    row = row.reshape(-1).astype(F32)
    col = col.reshape(-1).astype(F32)
    freqs = ROPE_BASE ** (-jnp.arange(0, AXIS_DIM, 2, dtype=F32) / AXIS_DIM)
    ang_r = row[:, None] * freqs
    ang_c = col[:, None] * freqs
    ang = jnp.concatenate([ang_r, ang_r, ang_c, ang_c], axis=-1)
    cos, sin = jnp.cos(ang), jnp.sin(ang)
    first = (jnp.arange(ROPE_DIM) & 16) == 0
    sin_signed = jnp.where(first[None, :], -sin, sin)
    ones, zeros = jnp.ones_like(cos), jnp.zeros_like(sin)
    cos2 = jnp.concatenate([cos, cos], axis=-1)
    sin2 = jnp.concatenate([sin_signed, sin_signed], axis=-1)
    cos_pe = jnp.concatenate([cos, ones], axis=-1)
    sin_pe = jnp.concatenate([sin_signed, zeros], axis=-1)
    return cos2, sin2, cos_pe, sin_pe


def kernel(x, c, ctx, c_ctx, w_ada, b_ada, norm1_g, norm2_g, w_in, b_gate, q_a_norm, w_uq, kv_a_norm, w_ukv,
           diff_lambda, diff_subln, w_o_diff, w_o_mla, w_out, w_mlp1, w_mlp2, final_norm_g):
    cos2, sin2, cos_pe, sin_pe = _rope_tables()

    o_q, o_k, o_v, o_cq = 0, DIFF_W, 2 * DIFF_W, 3 * DIFF_W
    o_g = o_cq + MLA_Q_RANK + MLA_KV_RANK + MLA_ROPE
    w_qkv = jnp.concatenate([w_in[:, :, o_q:o_k] * DIFF_SCALE, w_in[:, :, o_k:o_cq]], axis=-1).astype(BF16)
    w_lat = jnp.pad(w_in[:, :, o_cq:o_g], ((0, 0), (0, 0), (0, LAT_W - (o_g - o_cq)))).astype(BF16)
    w_gate = w_in[:, :, o_g:].astype(BF16)
    w_uq_p = jnp.pad(w_uq.reshape(DEPTH, MLA_Q_RANK, MLA_HEADS, MLA_NOPE + MLA_ROPE),
                     ((0, 0), (0, 0), (0, 0), (0, MLA_QK_PAD - MLA_NOPE - MLA_ROPE))
                     ).reshape(DEPTH, MLA_Q_RANK, MLA_HEADS * MLA_QK_PAD).astype(BF16)
    w_ukv4 = w_ukv.reshape(DEPTH, MLA_KV_RANK, MLA_HEADS, MLA_NOPE + MLA_VD)
    w_ukv_p = jnp.concatenate([w_ukv4[..., :MLA_NOPE].reshape(DEPTH, MLA_KV_RANK, MLA_W),
                               w_ukv4[..., MLA_NOPE:].reshape(DEPTH, MLA_KV_RANK, MLA_W)], axis=-1).astype(BF16)
    w_od = w_o_diff.astype(BF16)
    w_om = w_o_mla.astype(BF16)
    w_o = w_out.astype(BF16)
    w1 = w_mlp1.astype(BF16)
    w2 = w_mlp2.astype(BF16)

    cond = jnp.concatenate([c_ctx[None, :], c, jnp.zeros((MOD_ROWS - 1 - BATCH, D_MODEL), F32)], axis=0)
    mods = _ada_table(cond, w_ada, b_ada).reshape(DEPTH, MOD_ROWS, 1, N_MOD * D_MODEL)

    xs, h = _embed(ctx.reshape(M_CTX, D_MODEL), x.reshape(M_LAT, D_MODEL), norm1_g[0:1], mods[0])

    out = None
    for l in range(DEPTH):
        last = l == DEPTH - 1
        lam_init = 0.8 - 0.6 * math.exp(-0.3 * l)
        row0 = M_CTX if last else 0

        qkv = _qkv_proj(h, w_qkv[l], cos2, sin2)
        qm, km, vm = _lat_proj(h, w_lat[l], q_a_norm[l:l + 1], kv_a_norm[l:l + 1], w_uq_p[l], w_ukv_p[l],
                               cos_pe, sin_pe)
        gates = _gate_proj(h, w_gate[l], b_gate[l:l + 1])

        yd = _diff_attn(qkv, diff_lambda[l], diff_subln[l:l + 1], lam_init, out_row0=M_CTX - row0)
        ym = _mla_attn(qm, km, vm, out_row0=M_CTX - row0)
        if not last:
            yd = _diff_attn(qkv, diff_lambda[l], diff_subln[l:l + 1], lam_init, prev=yd)
            ym = _mla_attn(qm, km, vm, prev=ym)

        merged = _merge(yd, ym, w_od[l], w_om[l], gates, row0)
        x1, h2 = _out_proj(merged, w_o[l], xs, norm2_g[l:l + 1], mods[l], row0)
        if last:
            (out,) = _mlp(h2, w1[l], w2[l], x1, mods[l], final_norm_g[None, :], mods[l], row0, True)
        else:
            xs, h = _mlp(h2, w1[l], w2[l], x1, mods[l], norm1_g[l + 1:l + 2], mods[l + 1], row0, False)
    return out.reshape(BATCH, SEQ, D_MODEL)
```

```python
import functools
import math

import jax
import jax.numpy as jnp
from jax import lax
from jax.experimental import pallas as pl
from jax.experimental.pallas import tpu as pltpu

D_MODEL = 2048
BATCH = 16
SEQ = 2048
DEPTH = 4
CTX_LEN = 256
GRID_W = 64
ROPE_BASE = 10000.0
EPS = 1e-6

DIFF_HEADS = 8
DIFF_HD = 64
DIFF_VD = 2 * DIFF_HD
DIFF_W = DIFF_HEADS * DIFF_VD
DIFF_SCALE = DIFF_HD ** -0.5

MLA_HEADS = 8
MLA_Q_RANK = 512
MLA_KV_RANK = 256
MLA_NOPE = 128
MLA_ROPE = 64
MLA_VD = 128
MLA_W = MLA_HEADS * MLA_VD
MLA_SCALE = (MLA_NOPE + MLA_ROPE) ** -0.5
MLA_QK_PAD = 256

ROPE_DIM = 64
AXIS_DIM = ROPE_DIM // 2
D_FF = 4 * D_MODEL
N_MOD = 6

M_CTX = BATCH * CTX_LEN
M_LAT = BATCH * SEQ
M_ALL = M_CTX + M_LAT
MOD_ROWS = 24

DIFF_TQ, DIFF_QC = 1024, 256
MLA_TQ, MLA_QC = 1024, 256

LANES = 128
LAT_W = 1024

F32 = jnp.float32
BF16 = jnp.bfloat16

V7X_VMEM_BYTES = 64 * 1024 * 1024
VMEM_LIMIT = V7X_VMEM_BYTES - 8 * 1024 * 1024


def _params(sem):
    return pltpu.CompilerParams(dimension_semantics=sem, vmem_limit_bytes=VMEM_LIMIT)


def _rmsnorm(x, g):
    return x * lax.rsqrt(jnp.mean(x * x, axis=-1, keepdims=True) + EPS) * g


def _norm_modulate(x, g, shift, scale):
    return _rmsnorm(x, g) * (1.0 + scale) + shift


def _rope128(x, cos, sin_signed):
    lane = lax.broadcasted_iota(jnp.int32, x.shape, 1)
    first = (lane & 16) == 0
    rot = jnp.where(first, pltpu.roll(x, LANES - 16, 1), pltpu.roll(x, 16, 1))
    return x * cos + rot * sin_signed


def _mod_row(i, tiles_ctx, tiles_per_batch):
    return jnp.where(i < tiles_ctx, 0, 1 + (i - tiles_ctx) // tiles_per_batch)


def _ada_kernel(c_ref, w_ref, b_ref, o_ref):
    c = c_ref[...]
    s = (c * jax.nn.sigmoid(c)).astype(BF16)
    o_ref[...] = jnp.dot(s, w_ref[...].astype(BF16), preferred_element_type=F32) + b_ref[...]


def _ada_table(cond, w_ada, b_ada):
    tn = 1024
    n = N_MOD * D_MODEL
    return pl.pallas_call(
        _ada_kernel,
        grid=(DEPTH, n // tn),
        in_specs=[
            pl.BlockSpec((MOD_ROWS, D_MODEL), lambda l, j: (0, 0)),
            pl.BlockSpec((None, D_MODEL, tn), lambda l, j: (l, 0, j)),
            pl.BlockSpec((None, 1, tn), lambda l, j: (l, 0, j)),
        ],
        out_specs=pl.BlockSpec((None, MOD_ROWS, tn), lambda l, j: (l, 0, j)),
        out_shape=jax.ShapeDtypeStruct((DEPTH, MOD_ROWS, n), F32),
        compiler_params=_params(("parallel", "parallel")),
        name="ada_table",
    )(cond, w_ada, b_ada.reshape(DEPTH, 1, n))


def _embed_kernel(ctx_ref, x_ref, g_ref, mod_ref, xo_ref, h_ref, *, tiles_ctx):
    i = pl.program_id(0)
    mod = mod_ref[...]
    shift = mod[:, 0:D_MODEL]
    scale = mod[:, D_MODEL:2 * D_MODEL]

    def emit(src):
        xo_ref[...] = src
        h_ref[...] = _norm_modulate(src, g_ref[...], shift, scale).astype(BF16)

    @pl.when(i < tiles_ctx)
    def _():
        emit(ctx_ref[...])

    @pl.when(i >= tiles_ctx)
    def _():
        emit(x_ref[...])


def _embed(ctx2d, x2d, g, mods):
    tm = 512
    tiles_ctx = M_CTX // tm
    tpb = SEQ // tm
    return pl.pallas_call(
        functools.partial(_embed_kernel, tiles_ctx=tiles_ctx),
        grid=(M_ALL // tm,),
        in_specs=[
            pl.BlockSpec((tm, D_MODEL), lambda i: (jnp.minimum(i, tiles_ctx - 1), 0)),
            pl.BlockSpec((tm, D_MODEL), lambda i: (jnp.maximum(i - tiles_ctx, 0), 0)),
            pl.BlockSpec((1, D_MODEL), lambda i: (0, 0)),
            pl.BlockSpec((None, 1, N_MOD * D_MODEL), lambda i: (_mod_row(i, tiles_ctx, tpb), 0, 0)),
        ],
        out_specs=[
            pl.BlockSpec((tm, D_MODEL), lambda i: (i, 0)),
            pl.BlockSpec((tm, D_MODEL), lambda i: (i, 0)),
        ],
        out_shape=[
            jax.ShapeDtypeStruct((M_ALL, D_MODEL), F32),
            jax.ShapeDtypeStruct((M_ALL, D_MODEL), BF16),
        ],
        compiler_params=_params(("parallel",)),
        name="embed",
    )(ctx2d, x2d, g, mods)


def _qkv_kernel(h_ref, w_ref, cos_ref, sin_ref, o_ref, *, tiles_ctx, tn):
    i = pl.program_id(0)
    j = pl.program_id(1)
    acc = jnp.dot(h_ref[...], w_ref[...], preferred_element_type=F32)
    rotate = jnp.logical_and(i >= tiles_ctx, j < 2)

    @pl.when(rotate)
    def _():
        cos = cos_ref[...]
        sin = sin_ref[...]
        for c in range(tn // LANES):
            sl = slice(c * LANES, (c + 1) * LANES)
            o_ref[:, sl] = _rope128(acc[:, sl], cos, sin).astype(BF16)

    @pl.when(jnp.logical_not(rotate))
    def _():
        o_ref[...] = acc.astype(BF16)


def _qkv_proj(h, w_qkv, cos2, sin2):
    tm, tn = 1024, 1024
    tiles_ctx = M_CTX // tm
    tpb = SEQ // tm
    pos = lambda i, j: (jnp.maximum(i - tiles_ctx, 0) % tpb, 0)
    return pl.pallas_call(
        functools.partial(_qkv_kernel, tiles_ctx=tiles_ctx, tn=tn),
        grid=(M_ALL // tm, 3 * DIFF_W // tn),
        in_specs=[
            pl.BlockSpec((tm, D_MODEL), lambda i, j: (i, 0)),
            pl.BlockSpec((D_MODEL, tn), lambda i, j: (0, j)),
            pl.BlockSpec((tm, LANES), pos),
            pl.BlockSpec((tm, LANES), pos),
        ],
        out_specs=pl.BlockSpec((tm, tn), lambda i, j: (i, j)),
        out_shape=jax.ShapeDtypeStruct((M_ALL, 3 * DIFF_W), BF16),
        compiler_params=_params(("parallel", "parallel")),
        name="qkv_proj",
    )(h, w_qkv, cos2, sin2)


def _lat_kernel(h_ref, wlat_ref, qg_ref, kvg_ref, wuq_ref, wukv_ref, cos_ref, sin_ref,
                qm_ref, km_ref, vm_ref, *, tiles_ctx):
    i = pl.program_id(0)
    is_lat = i >= tiles_ctx
    acc = jnp.dot(h_ref[...], wlat_ref[...], preferred_element_type=F32)
    cq = acc[:, 0:MLA_Q_RANK]
    ckv = acc[:, MLA_Q_RANK:MLA_Q_RANK + MLA_KV_RANK]
    kr = acc[:, MLA_Q_RANK + MLA_KV_RANK:MLA_Q_RANK + MLA_KV_RANK + LANES]
    cos = cos_ref[...]
    sin = sin_ref[...]

    cqn = _rmsnorm(cq, qg_ref[...]).astype(BF16)
    qm = jnp.dot(cqn, wuq_ref[...], preferred_element_type=F32) * MLA_SCALE
    ckvn = _rmsnorm(ckv, kvg_ref[...]).astype(BF16)
    kv = jnp.dot(ckvn, wukv_ref[...], preferred_element_type=F32)
    kpe = jnp.where(is_lat, _rope128(kr, cos, sin), kr).astype(BF16)

    for hd in range(MLA_HEADS):
        q0 = hd * MLA_QK_PAD
        qm_ref[:, q0:q0 + LANES] = qm[:, q0:q0 + LANES].astype(BF16)
        pe = qm[:, q0 + LANES:q0 + 2 * LANES]
        qm_ref[:, q0 + LANES:q0 + 2 * LANES] = jnp.where(is_lat, _rope128(pe, cos, sin), pe).astype(BF16)
        km_ref[:, q0:q0 + LANES] = kv[:, hd * LANES:(hd + 1) * LANES].astype(BF16)
        km_ref[:, q0 + LANES:q0 + 2 * LANES] = kpe
    vm_ref[...] = kv[:, MLA_W:2 * MLA_W].astype(BF16)


def _lat_proj(h, w_lat, qg, kvg, w_uq, w_ukv, cos_pe, sin_pe):
    tm = 512
    tiles_ctx = M_CTX // tm
    tpb = SEQ // tm
    pos = lambda i: (jnp.maximum(i - tiles_ctx, 0) % tpb, 0)
    const = lambda i: (0, 0)
    qk_w = MLA_HEADS * MLA_QK_PAD
    return pl.pallas_call(
        functools.partial(_lat_kernel, tiles_ctx=tiles_ctx),
        grid=(M_ALL // tm,),
        in_specs=[
            pl.BlockSpec((tm, D_MODEL), lambda i: (i, 0)),
            pl.BlockSpec((D_MODEL, LAT_W), const),
            pl.BlockSpec((1, MLA_Q_RANK), const),
            pl.BlockSpec((1, MLA_KV_RANK), const),
            pl.BlockSpec((MLA_Q_RANK, qk_w), const),
            pl.BlockSpec((MLA_KV_RANK, 2 * MLA_W), const),
            pl.BlockSpec((tm, LANES), pos),
            pl.BlockSpec((tm, LANES), pos),
        ],
        out_specs=[
            pl.BlockSpec((tm, qk_w), lambda i: (i, 0)),
            pl.BlockSpec((tm, qk_w), lambda i: (i, 0)),
            pl.BlockSpec((tm, MLA_W), lambda i: (i, 0)),
        ],
        out_shape=[
            jax.ShapeDtypeStruct((M_ALL, qk_w), BF16),
            jax.ShapeDtypeStruct((M_ALL, qk_w), BF16),
            jax.ShapeDtypeStruct((M_ALL, MLA_W), BF16),
        ],
        compiler_params=_params(("parallel",)),
        name="lat_proj",
    )(h, w_lat, qg, kvg, w_uq, w_ukv, cos_pe, sin_pe)


def _gate_kernel(h_ref, w_ref, b_ref, o_ref):
    acc = jnp.dot(h_ref[...], w_ref[...], preferred_element_type=F32)
    o_ref[...] = jax.nn.sigmoid(acc + b_ref[...]).astype(BF16)


def _gate_proj(h, w_gate, b_gate):
    tm, tn = 1024, 1024
    n = 2 * D_MODEL
    return pl.pallas_call(
        _gate_kernel,
        grid=(M_ALL // tm, n // tn),
        in_specs=[
            pl.BlockSpec((tm, D_MODEL), lambda i, j: (i, 0)),
            pl.BlockSpec((D_MODEL, tn), lambda i, j: (0, j)),
            pl.BlockSpec((1, tn), lambda i, j: (0, j)),
        ],
        out_specs=pl.BlockSpec((tm, tn), lambda i, j: (i, j)),
        out_shape=jax.ShapeDtypeStruct((M_ALL, n), BF16),
        compiler_params=_params(("parallel", "parallel")),
        name="gate_proj",
    )(h, w_gate, b_gate)


def _softmax_pv(q, k_refs, v_refs):
    dims = (((1,), (1,)), ((), ()))
    s = [lax.dot_general(q, k[...], dims, preferred_element_type=F32) for k in k_refs]
    m = s[0].max(axis=1, keepdims=True)
    for t in s[1:]:
        m = jnp.maximum(m, t.max(axis=1, keepdims=True))
    p = [jnp.exp(t - m) for t in s]
    l = p[0].sum(axis=1, keepdims=True)
    for t in p[1:]:
        l = l + t.sum(axis=1, keepdims=True)
    o = jnp.dot(p[0].astype(BF16), v_refs[0][...], preferred_element_type=F32)
    for t, v in zip(p[1:], v_refs[1:]):
        o = o + jnp.dot(t.astype(BF16), v[...], preferred_element_type=F32)
    return o / l


def _diff_attn_kernel(*refs, n_kv, lam_init, qc):
    dl_ref, g_ref, q_ref = refs[0:3]
    k_refs = refs[3:3 + n_kv]
    v_refs = refs[3 + n_kv:3 + 2 * n_kv]
    o_ref = refs[3 + 2 * n_kv]
    tq = q_ref.shape[0]
    dl = dl_ref[...]
    lam = (jnp.exp(jnp.sum(dl[0:1, :] * dl[1:2, :], axis=1, keepdims=True))
           - jnp.exp(jnp.sum(dl[2:3, :] * dl[3:4, :], axis=1, keepdims=True)) + lam_init)
    for c in range(tq // qc):
        q = q_ref[c * qc:(c + 1) * qc, :].astype(F32)
        lane = lax.broadcasted_iota(jnp.int32, q.shape, 1)
        q2 = jnp.concatenate([jnp.where(lane < DIFF_HD, q, 0.0), jnp.where(lane >= DIFF_HD, q, 0.0)],
                             axis=0).astype(BF16)
        o = _softmax_pv(q2, k_refs, v_refs)
        y = o[0:qc] - lam * o[qc:2 * qc]
        o_ref[c * qc:(c + 1) * qc, :] = (_rmsnorm(y, g_ref[...]) * (1.0 - lam_init)).astype(BF16)


def _mla_attn_kernel(*refs, n_kv, qc):
    q_ref = refs[0]
    k_refs = refs[1:1 + n_kv]
    v_refs = refs[1 + n_kv:1 + 2 * n_kv]
    o_ref = refs[1 + 2 * n_kv]
    for c in range(q_ref.shape[0] // qc):
        rows = slice(c * qc, (c + 1) * qc)
        o_ref[rows, :] = _softmax_pv(q_ref[rows, :], k_refs, v_refs).astype(BF16)


def _attn_specs(tq, dk, q_col0, k_col0, v_col0, latent, out_row0):
    if latent:
        row0 = M_CTX // tq
        orow0 = out_row0 // tq
        per_b = SEQ // tq
        q_map = lambda b, h, i: (row0 + b * per_b + i, q_col0 + h)
        o_map = lambda b, h, i: (orow0 + b * per_b + i, h)
    else:
        q_map = lambda b, h, i: (b, q_col0 + h)
        o_map = lambda b, h, i: (b, h)
    q_spec = pl.BlockSpec((tq, dk), q_map)
    kc = pl.BlockSpec((CTX_LEN, dk), lambda b, h, i: (b, k_col0 + h))
    vc = pl.BlockSpec((CTX_LEN, LANES), lambda b, h, i: (b, v_col0 + h))
    k_specs, v_specs = [kc], [vc]
    if latent:
        lat0 = M_CTX // SEQ
        k_specs.append(pl.BlockSpec((SEQ, dk), lambda b, h, i: (lat0 + b, k_col0 + h)))
        v_specs.append(pl.BlockSpec((SEQ, LANES), lambda b, h, i: (lat0 + b, v_col0 + h)))
    o_spec = pl.BlockSpec((tq, LANES), o_map)
    return q_spec, k_specs, v_specs, o_spec


def _diff_attn(qkv, dl, g, lam_init, out_row0=M_CTX, prev=None):
    latent = prev is None
    tq = DIFF_TQ if latent else CTX_LEN
    nq = SEQ // tq if latent else 1
    q_spec, k_specs, v_specs, o_spec = _attn_specs(tq, LANES, 0, DIFF_HEADS, 2 * DIFF_HEADS, latent, out_row0)
    n_kv = len(k_specs)
    const = lambda b, h, i: (0, 0)
    in_specs = [pl.BlockSpec((4, DIFF_HD), const), pl.BlockSpec((1, DIFF_VD), const), q_spec] + k_specs + v_specs
    args = [dl, g, qkv] + [qkv] * (2 * n_kv)
    aliases = {}
    if not latent:
        in_specs.append(pl.BlockSpec(memory_space=pl.ANY))
        args.append(prev)
        aliases = {len(args) - 1: 0}
    kern = functools.partial(_diff_attn_kernel, n_kv=n_kv, lam_init=lam_init, qc=DIFF_QC)
    if not latent:
        kern = functools.partial(_drop_last_input, kern, n_in=len(args))
    return pl.pallas_call(
        kern,
        grid=(BATCH, DIFF_HEADS, nq),
        in_specs=in_specs,
        out_specs=o_spec,
        out_shape=jax.ShapeDtypeStruct((out_row0 + M_LAT, DIFF_W), BF16),
        input_output_aliases=aliases,
        compiler_params=_params(("parallel", "parallel", "parallel")),
        name="diff_attn_lat" if latent else "diff_attn_ctx",
    )(*args)


def _drop_last_input(kern, *refs, n_in):
    kern(*refs[:n_in - 1], *refs[n_in:])


def _mla_attn(qm, km, vm, out_row0=M_CTX, prev=None):
    latent = prev is None
    tq = MLA_TQ if latent else CTX_LEN
    nq = SEQ // tq if latent else 1
    q_spec, k_specs, v_specs, o_spec = _attn_specs(tq, MLA_QK_PAD, 0, 0, 0, latent, out_row0)
    n_kv = len(k_specs)
    in_specs = [q_spec] + k_specs + v_specs
    args = [qm] + [km] * n_kv + [vm] * n_kv
    aliases = {}
    if not latent:
        in_specs.append(pl.BlockSpec(memory_space=pl.ANY))
        args.append(prev)
        aliases = {len(args) - 1: 0}
    kern = functools.partial(_mla_attn_kernel, n_kv=n_kv, qc=MLA_QC)
    if not latent:
        kern = functools.partial(_drop_last_input, kern, n_in=len(args))
    return pl.pallas_call(
        kern,
        grid=(BATCH, MLA_HEADS, nq),
        in_specs=in_specs,
        out_specs=o_spec,
        out_shape=jax.ShapeDtypeStruct((out_row0 + M_LAT, MLA_W), BF16),
        input_output_aliases=aliases,
        compiler_params=_params(("parallel", "parallel", "parallel")),
        name="mla_attn_lat" if latent else "mla_attn_ctx",
    )(*args)


def _merge_kernel(yd_ref, ym_ref, wd_ref, wm_ref, gd_ref, gm_ref, o_ref):
    yd = jnp.dot(yd_ref[...], wd_ref[...], preferred_element_type=F32)
    ym = jnp.dot(ym_ref[...], wm_ref[...], preferred_element_type=F32)
    o_ref[...] = (gd_ref[...].astype(F32) * yd + gm_ref[...].astype(F32) * ym).astype(BF16)


def _merge(yd, ym, w_od, w_om, gates, row0):
    tm, tn = 1024, 1024
    t0 = row0 // tm
    nj = D_MODEL // tn
    return pl.pallas_call(
        _merge_kernel,
        grid=((M_ALL - row0) // tm, nj),
        in_specs=[
            pl.BlockSpec((tm, DIFF_W), lambda i, j: (i, 0)),
            pl.BlockSpec((tm, MLA_W), lambda i, j: (i, 0)),
            pl.BlockSpec((DIFF_W, tn), lambda i, j: (0, j)),
            pl.BlockSpec((MLA_W, tn), lambda i, j: (0, j)),
            pl.BlockSpec((tm, tn), lambda i, j: (t0 + i, j)),
            pl.BlockSpec((tm, tn), lambda i, j: (t0 + i, nj + j)),
        ],
        out_specs=pl.BlockSpec((tm, tn), lambda i, j: (i, j)),
        out_shape=jax.ShapeDtypeStruct((M_ALL - row0, D_MODEL), BF16),
        compiler_params=_params(("parallel", "parallel")),
        name="merge",
    )(yd, ym, w_od, w_om, gates, gates)


def _out_kernel(a_ref, w_ref, x_ref, g_ref, mod_ref, xo_ref, h_ref):
    mod = mod_ref[...]
    gate1 = mod[:, 2 * D_MODEL:3 * D_MODEL]
    shift2 = mod[:, 3 * D_MODEL:4 * D_MODEL]
    scale2 = mod[:, 4 * D_MODEL:5 * D_MODEL]
    x1 = x_ref[...] + gate1 * jnp.dot(a_ref[...], w_ref[...], preferred_element_type=F32)
    xo_ref[...] = x1
    h_ref[...] = _norm_modulate(x1, g_ref[...], shift2, scale2).astype(BF16)


def _out_proj(merged, w_out, x, g2, mods, row0):
    tm = 512
    t0 = row0 // tm
    tiles_ctx = M_CTX // tm
    tpb = SEQ // tm
    rows = lambda i: (i, 0)
    const = lambda i: (0, 0)
    return pl.pallas_call(
        _out_kernel,
        grid=((M_ALL - row0) // tm,),
        in_specs=[
            pl.BlockSpec((tm, D_MODEL), rows),
            pl.BlockSpec((D_MODEL, D_MODEL), const),
            pl.BlockSpec((tm, D_MODEL), lambda i: (t0 + i, 0)),
            pl.BlockSpec((1, D_MODEL), const),
            pl.BlockSpec((None, 1, N_MOD * D_MODEL), lambda i: (_mod_row(t0 + i, tiles_ctx, tpb), 0, 0)),
        ],
        out_specs=[pl.BlockSpec((tm, D_MODEL), rows), pl.BlockSpec((tm, D_MODEL), rows)],
        out_shape=[
            jax.ShapeDtypeStruct((M_ALL - row0, D_MODEL), F32),
            jax.ShapeDtypeStruct((M_ALL - row0, D_MODEL), BF16),
        ],
        compiler_params=_params(("parallel",)),
        name="out_proj",
    )(merged, w_out, x, g2, mods)


def _mlp_kernel(h_ref, w1_ref, w2_ref, x_ref, mod_ref, gn_ref, modn_ref, *out_refs, final):
    acc_ref = out_refs[-1]
    f = pl.program_id(1)

    @pl.when(f == 0)
    def _():
        acc_ref[...] = jnp.zeros_like(acc_ref)

    u = jnp.maximum(jnp.dot(h_ref[...], w1_ref[...], preferred_element_type=F32), 0.0)
    acc_ref[...] += jnp.dot((u * u).astype(BF16), w2_ref[...], preferred_element_type=F32)

    @pl.when(f == pl.num_programs(1) - 1)
    def _():
        gate2 = mod_ref[...][:, 5 * D_MODEL:6 * D_MODEL]
        x2 = x_ref[...] + gate2 * acc_ref[...]
        if final:
            out_refs[0][...] = _rmsnorm(x2, gn_ref[...])
        else:
            modn = modn_ref[...]
            out_refs[0][...] = x2
            out_refs[1][...] = _norm_modulate(
                x2, gn_ref[...], modn[:, 0:D_MODEL], modn[:, D_MODEL:2 * D_MODEL]).astype(BF16)


def _mlp(h2, w1, w2, x1, mods, g_next, mods_next, row0, final):
    tm, tf = 512, 1024
    t0 = row0 // tm
    tiles_ctx = M_CTX // tm
    tpb = SEQ // tm
    rows = lambda i, f: (i, 0)
    mod_map = lambda i, f: (_mod_row(t0 + i, tiles_ctx, tpb), 0, 0)
    if final:
        out_specs = [pl.BlockSpec((tm, D_MODEL), rows)]
        out_shape = [jax.ShapeDtypeStruct((M_ALL - row0, D_MODEL), F32)]
    else:
        out_specs = [pl.BlockSpec((tm, D_MODEL), rows), pl.BlockSpec((tm, D_MODEL), rows)]
        out_shape = [jax.ShapeDtypeStruct((M_ALL - row0, D_MODEL), F32),
                     jax.ShapeDtypeStruct((M_ALL - row0, D_MODEL), BF16)]
    return pl.pallas_call(
        functools.partial(_mlp_kernel, final=final),
        grid=((M_ALL - row0) // tm, D_FF // tf),
        in_specs=[
            pl.BlockSpec((tm, D_MODEL), rows),
            pl.BlockSpec((D_MODEL, tf), lambda i, f: (0, f)),
            pl.BlockSpec((tf, D_MODEL), lambda i, f: (f, 0)),
            pl.BlockSpec((tm, D_MODEL), rows),
            pl.BlockSpec((None, 1, N_MOD * D_MODEL), mod_map),
            pl.BlockSpec((1, D_MODEL), lambda i, f: (0, 0)),
            pl.BlockSpec((None, 1, N_MOD * D_MODEL), mod_map),
        ],
        out_specs=out_specs,
        out_shape=out_shape,
        scratch_shapes=[pltpu.VMEM((tm, D_MODEL), F32)],
        compiler_params=_params(("parallel", "arbitrary")),
        name="mlp_final" if final else "mlp",
    )(h2, w1, w2, x1, mods, g_next, mods_next)


def _rope_tables():
    rows = SEQ // GRID_W
    row, col = jnp.meshgrid(jnp.arange(rows), jnp.arange(GRID_W), indexing="ij")
    row = row.reshape(-1).astype(F32)
    col = col.reshape(-1).astype(F32)
    freqs = ROPE_BASE ** (-jnp.arange(0, AXIS_DIM, 2, dtype=F32) / AXIS_DIM)
    ang_r = row[:, None] * freqs
    ang_c = col[:, None] * freqs
    ang = jnp.concatenate([ang_r, ang_r, ang_c, ang_c], axis=-1)
    cos, sin = jnp.cos(ang), jnp.sin(ang)
    first = (jnp.arange(ROPE_DIM) & 16) == 0
    sin_signed = jnp.where(first[None, :], -sin, sin)
    ones, zeros = jnp.ones_like(cos), jnp.zeros_like(sin)
    cos2 = jnp.concatenate([cos, cos], axis=-1)
    sin2 = jnp.concatenate([sin_signed, sin_signed], axis=-1)
    cos_pe = jnp.concatenate([cos, ones], axis=-1)
    sin_pe = jnp.concatenate([sin_signed, zeros], axis=-1)
    return cos2, sin2, cos_pe, sin_pe


def kernel(x, c, ctx, c_ctx, w_ada, b_ada, norm1_g, norm2_g, w_in, b_gate, q_a_norm, w_uq, kv_a_norm, w_ukv,
           diff_lambda, diff_subln, w_o_diff, w_o_mla, w_out, w_mlp1, w_mlp2, final_norm_g):
    cos2, sin2, cos_pe, sin_pe = _rope_tables()

    o_q, o_k, o_v, o_cq = 0, DIFF_W, 2 * DIFF_W, 3 * DIFF_W
    o_g = o_cq + MLA_Q_RANK + MLA_KV_RANK + MLA_ROPE
    w_qkv = jnp.concatenate([w_in[:, :, o_q:o_k] * DIFF_SCALE, w_in[:, :, o_k:o_cq]], axis=-1).astype(BF16)
    w_lat = jnp.pad(w_in[:, :, o_cq:o_g], ((0, 0), (0, 0), (0, LAT_W - (o_g - o_cq)))).astype(BF16)
    w_gate = w_in[:, :, o_g:].astype(BF16)
    w_uq_p = jnp.pad(w_uq.reshape(DEPTH, MLA_Q_RANK, MLA_HEADS, MLA_NOPE + MLA_ROPE),
                     ((0, 0), (0, 0), (0, 0), (0, MLA_QK_PAD - MLA_NOPE - MLA_ROPE))
                     ).reshape(DEPTH, MLA_Q_RANK, MLA_HEADS * MLA_QK_PAD).astype(BF16)
    w_ukv4 = w_ukv.reshape(DEPTH, MLA_KV_RANK, MLA_HEADS, MLA_NOPE + MLA_VD)
    w_ukv_p = jnp.concatenate([w_ukv4[..., :MLA_NOPE].reshape(DEPTH, MLA_KV_RANK, MLA_W),
                               w_ukv4[..., MLA_NOPE:].reshape(DEPTH, MLA_KV_RANK, MLA_W)], axis=-1).astype(BF16)
    w_od = w_o_diff.astype(BF16)
    w_om = w_o_mla.astype(BF16)
    w_o = w_out.astype(BF16)
    w1 = w_mlp1.astype(BF16)
    w2 = w_mlp2.astype(BF16)

    cond = jnp.concatenate([c_ctx[None, :], c, jnp.zeros((MOD_ROWS - 1 - BATCH, D_MODEL), F32)], axis=0)
    mods = _ada_table(cond, w_ada, b_ada).reshape(DEPTH, MOD_ROWS, 1, N_MOD * D_MODEL)

    xs, h = _embed(ctx.reshape(M_CTX, D_MODEL), x.reshape(M_LAT, D_MODEL), norm1_g[0:1], mods[0])

    out = None
    for l in range(DEPTH):
        last = l == DEPTH - 1
        lam_init = 0.8 - 0.6 * math.exp(-0.3 * l)
        row0 = M_CTX if last else 0

        qkv = _qkv_proj(h, w_qkv[l], cos2, sin2)
        qm, km, vm = _lat_proj(h, w_lat[l], q_a_norm[l:l + 1], kv_a_norm[l:l + 1], w_uq_p[l], w_ukv_p[l],
                               cos_pe, sin_pe)
        gates = _gate_proj(h, w_gate[l], b_gate[l:l + 1])

        yd = _diff_attn(qkv, diff_lambda[l], diff_subln[l:l + 1], lam_init, out_row0=M_CTX - row0)
        ym = _mla_attn(qm, km, vm, out_row0=M_CTX - row0)
        if not last:
            yd = _diff_attn(qkv, diff_lambda[l], diff_subln[l:l + 1], lam_init, prev=yd)
            ym = _mla_attn(qm, km, vm, prev=ym)

        merged = _merge(yd, ym, w_od[l], w_om[l], gates, row0)
        x1, h2 = _out_proj(merged, w_o[l], xs, norm2_g[l:l + 1], mods[l], row0)
        if last:
            (out,) = _mlp(h2, w1[l], w2[l], x1, mods[l], final_norm_g[None, :], mods[l], row0, True)
        else:
            xs, h = _mlp(h2, w1[l], w2[l], x1, mods[l], norm1_g[l + 1:l + 2], mods[l + 1], row0, False)
    return out.reshape(BATCH, SEQ, D_MODEL)
```

```python
import functools
import math

import jax
import jax.numpy as jnp
from jax import lax
from jax.experimental import pallas as pl
from jax.experimental.pallas import tpu as pltpu

D_MODEL = 2048
BATCH = 16
SEQ = 2048
DEPTH = 4
CTX_LEN = 256
GRID_W = 64
ROPE_BASE = 10000.0
EPS = 1e-6

DIFF_HEADS = 8
DIFF_HD = 64
DIFF_VD = 2 * DIFF_HD
DIFF_W = DIFF_HEADS * DIFF_VD
DIFF_SCALE = DIFF_HD ** -0.5

MLA_HEADS = 8
MLA_Q_RANK = 512
MLA_KV_RANK = 256
MLA_NOPE = 128
MLA_ROPE = 64
MLA_VD = 128
MLA_W = MLA_HEADS * MLA_VD
MLA_SCALE = (MLA_NOPE + MLA_ROPE) ** -0.5
MLA_QK_PAD = 256

ROPE_DIM = 64
AXIS_DIM = ROPE_DIM // 2
D_FF = 4 * D_MODEL
N_MOD = 6

M_CTX = BATCH * CTX_LEN
M_LAT = BATCH * SEQ
M_ALL = M_CTX + M_LAT
MOD_ROWS = 24

ATTN_TQ, ATTN_QC = 1024, 256
LOG2E = math.log2(math.e)
QKV_TM = 1024

LANES = 128
LAT_W = 1024

F32 = jnp.float32
BF16 = jnp.bfloat16

V7X_VMEM_BYTES = 64 * 1024 * 1024
VMEM_LIMIT = V7X_VMEM_BYTES - 8 * 1024 * 1024


def _params(sem):
    return pltpu.CompilerParams(dimension_semantics=sem, vmem_limit_bytes=VMEM_LIMIT)


def _rmsnorm(x, g):
    return x * lax.rsqrt(jnp.mean(x * x, axis=-1, keepdims=True) + EPS) * g


def _norm_modulate(x, g, shift, scale):
    return _rmsnorm(x, g) * (1.0 + scale) + shift


def _rope128(x, cos, sin_signed):
    lane = lax.broadcasted_iota(jnp.int32, x.shape, 1)
    first = (lane & 16) == 0
    rot = jnp.where(first, pltpu.roll(x, LANES - 16, 1), pltpu.roll(x, 16, 1))
    return x * cos + rot * sin_signed


def _mod_row(i, tiles_ctx, tiles_per_batch):
    return jnp.where(i < tiles_ctx, 0, 1 + (i - tiles_ctx) // tiles_per_batch)


def _ada_kernel(c_ref, w_ref, b_ref, o_ref):
    c = c_ref[...]
    s = (c * jax.nn.sigmoid(c)).astype(BF16)
    o_ref[...] = jnp.dot(s, w_ref[...].astype(BF16), preferred_element_type=F32) + b_ref[...]


def _ada_table(cond, w_ada, b_ada):
    tn = 1024
    n = N_MOD * D_MODEL
    return pl.pallas_call(
        _ada_kernel,
        grid=(DEPTH, n // tn),
        in_specs=[
            pl.BlockSpec((MOD_ROWS, D_MODEL), lambda l, j: (0, 0)),
            pl.BlockSpec((None, D_MODEL, tn), lambda l, j: (l, 0, j)),
            pl.BlockSpec((None, 1, tn), lambda l, j: (l, 0, j)),
        ],
        out_specs=pl.BlockSpec((None, MOD_ROWS, tn), lambda l, j: (l, 0, j)),
        out_shape=jax.ShapeDtypeStruct((DEPTH, MOD_ROWS, n), F32),
        compiler_params=_params(("parallel", "parallel")),
        name="ada_table",
    )(cond, w_ada, b_ada.reshape(DEPTH, 1, n))


def _embed_kernel(ctx_ref, x_ref, g_ref, mod_ref, xo_ref, h_ref, *, tiles_ctx):
    i = pl.program_id(0)
    mod = mod_ref[...]
    shift = mod[:, 0:D_MODEL]
    scale = mod[:, D_MODEL:2 * D_MODEL]

    def emit(src):
        xo_ref[...] = src
        h_ref[...] = _norm_modulate(src, g_ref[...], shift, scale).astype(BF16)

    @pl.when(i < tiles_ctx)
    def _():
        emit(ctx_ref[...])

    @pl.when(i >= tiles_ctx)
    def _():
        emit(x_ref[...])


def _embed(ctx2d, x2d, g, mods):
    tm = 512
    tiles_ctx = M_CTX // tm
    tpb = SEQ // tm
    return pl.pallas_call(
        functools.partial(_embed_kernel, tiles_ctx=tiles_ctx),
        grid=(M_ALL // tm,),
        in_specs=[
            pl.BlockSpec((tm, D_MODEL), lambda i: (jnp.minimum(i, tiles_ctx - 1), 0)),
            pl.BlockSpec((tm, D_MODEL), lambda i: (jnp.maximum(i - tiles_ctx, 0), 0)),
            pl.BlockSpec((1, D_MODEL), lambda i: (0, 0)),
            pl.BlockSpec((None, 1, N_MOD * D_MODEL), lambda i: (_mod_row(i, tiles_ctx, tpb), 0, 0)),
        ],
        out_specs=[
            pl.BlockSpec((tm, D_MODEL), lambda i: (i, 0)),
            pl.BlockSpec((tm, D_MODEL), lambda i: (i, 0)),
        ],
        out_shape=[
            jax.ShapeDtypeStruct((M_ALL, D_MODEL), F32),
            jax.ShapeDtypeStruct((M_ALL, D_MODEL), BF16),
        ],
        compiler_params=_params(("parallel",)),
        name="embed",
    )(ctx2d, x2d, g, mods)


def _qkv_kernel(h_ref, w_ref, cos_ref, sin_ref, o_ref, *, tn, rc):
    for r in range(h_ref.shape[0] // rc):
        rows = slice(r * rc, (r + 1) * rc)
        acc = jnp.dot(h_ref[rows, :], w_ref[...], preferred_element_type=F32)
        cos = cos_ref[rows, :]
        sin = sin_ref[rows, :]
        for c in range(tn // LANES):
            sl = slice(c * LANES, (c + 1) * LANES)
            o_ref[rows, sl] = _rope128(acc[:, sl], cos, sin).astype(BF16)


def _qkv_proj(h, w_qkv, cos2, sin2):
    tm, tn = QKV_TM, 1024
    tiles_ctx = M_CTX // tm
    tpb = SEQ // tm

    def table_block(i, j):
        pos = jnp.maximum(i - tiles_ctx, 0) % tpb
        qk = jnp.where(i >= tiles_ctx, j * tpb + pos, 2 * tpb + j)
        return (jnp.where(j == 2, 2 * tpb + 1, qk), 0)

    return pl.pallas_call(
        functools.partial(_qkv_kernel, tn=tn, rc=256),
        grid=(M_ALL // tm, 3 * DIFF_W // tn),
        in_specs=[
            pl.BlockSpec((tm, D_MODEL), lambda i, j: (i, 0)),
            pl.BlockSpec((D_MODEL, tn), lambda i, j: (0, j)),
            pl.BlockSpec((tm, LANES), table_block),
            pl.BlockSpec((tm, LANES), table_block),
        ],
        out_specs=pl.BlockSpec((tm, tn), lambda i, j: (i, j)),
        out_shape=jax.ShapeDtypeStruct((M_ALL, 3 * DIFF_W), BF16),
        compiler_params=_params(("parallel", "parallel")),
        name="qkv_proj",
    )(h, w_qkv, cos2, sin2)


def _lat_kernel(h_ref, wlat_ref, qg_ref, kvg_ref, wuq_ref, wukv_ref, cos_ref, sin_ref,
                qm_ref, km_ref, vm_ref, *, tiles_ctx):
    i = pl.program_id(0)
    is_lat = i >= tiles_ctx
    acc = jnp.dot(h_ref[...], wlat_ref[...], preferred_element_type=F32)
    cq = acc[:, 0:MLA_Q_RANK]
    ckv = acc[:, MLA_Q_RANK:MLA_Q_RANK + MLA_KV_RANK]
    kr = acc[:, MLA_Q_RANK + MLA_KV_RANK:MLA_Q_RANK + MLA_KV_RANK + LANES]
    cos = cos_ref[...]
    sin = sin_ref[...]

    cqn = _rmsnorm(cq, qg_ref[...]).astype(BF16)
    qm = jnp.dot(cqn, wuq_ref[...], preferred_element_type=F32) * (MLA_SCALE * LOG2E)
    ckvn = _rmsnorm(ckv, kvg_ref[...]).astype(BF16)
    kv = jnp.dot(ckvn, wukv_ref[...], preferred_element_type=F32)
    kpe = jnp.where(is_lat, _rope128(kr, cos, sin), kr).astype(BF16)

    for hd in range(MLA_HEADS):
        q0 = hd * MLA_QK_PAD
        qm_ref[:, q0:q0 + LANES] = qm[:, q0:q0 + LANES].astype(BF16)
        pe = qm[:, q0 + LANES:q0 + 2 * LANES]
        qm_ref[:, q0 + LANES:q0 + 2 * LANES] = jnp.where(is_lat, _rope128(pe, cos, sin), pe).astype(BF16)
        km_ref[:, q0:q0 + LANES] = kv[:, hd * LANES:(hd + 1) * LANES].astype(BF16)
        km_ref[:, q0 + LANES:q0 + 2 * LANES] = kpe
    vm_ref[...] = kv[:, MLA_W:2 * MLA_W].astype(BF16)


def _lat_proj(h, w_lat, qg, kvg, w_uq, w_ukv, cos_pe, sin_pe):
    tm = 512
    tiles_ctx = M_CTX // tm
    tpb = SEQ // tm
    pos = lambda i: (jnp.maximum(i - tiles_ctx, 0) % tpb, 0)
    const = lambda i: (0, 0)
    qk_w = MLA_HEADS * MLA_QK_PAD
    return pl.pallas_call(
        functools.partial(_lat_kernel, tiles_ctx=tiles_ctx),
        grid=(M_ALL // tm,),
        in_specs=[
            pl.BlockSpec((tm, D_MODEL), lambda i: (i, 0)),
            pl.BlockSpec((D_MODEL, LAT_W), const),
            pl.BlockSpec((1, MLA_Q_RANK), const),
            pl.BlockSpec((1, MLA_KV_RANK), const),
            pl.BlockSpec((MLA_Q_RANK, qk_w), const),
            pl.BlockSpec((MLA_KV_RANK, 2 * MLA_W), const),
            pl.BlockSpec((tm, LANES), pos),
            pl.BlockSpec((tm, LANES), pos),
        ],
        out_specs=[
            pl.BlockSpec((tm, qk_w), lambda i: (i, 0)),
            pl.BlockSpec((tm, qk_w), lambda i: (i, 0)),
            pl.BlockSpec((tm, MLA_W), lambda i: (i, 0)),
        ],
        out_shape=[
            jax.ShapeDtypeStruct((M_ALL, qk_w), BF16),
            jax.ShapeDtypeStruct((M_ALL, qk_w), BF16),
            jax.ShapeDtypeStruct((M_ALL, MLA_W), BF16),
        ],
        compiler_params=_params(("parallel",)),
        name="lat_proj",
    )(h, w_lat, qg, kvg, w_uq, w_ukv, cos_pe, sin_pe)


def _gate_kernel(h_ref, w_ref, b_ref, o_ref):
    rc = 256
    for r in range(h_ref.shape[0] // rc):
        rows = slice(r * rc, (r + 1) * rc)
        acc = jnp.dot(h_ref[rows, :], w_ref[...], preferred_element_type=F32)
        o_ref[rows, :] = jax.nn.sigmoid(acc + b_ref[...]).astype(BF16)


def _gate_proj(h, w_gate, b_gate):
    tm, tn = 1024, 1024
    n = 2 * D_MODEL
    return pl.pallas_call(
        _gate_kernel,
        grid=(M_ALL // tm, n // tn),
        in_specs=[
            pl.BlockSpec((tm, D_MODEL), lambda i, j: (i, 0)),
            pl.BlockSpec((D_MODEL, tn), lambda i, j: (0, j)),
            pl.BlockSpec((1, tn), lambda i, j: (0, j)),
        ],
        out_specs=pl.BlockSpec((tm, tn), lambda i, j: (i, j)),
        out_shape=jax.ShapeDtypeStruct((M_ALL, n), BF16),
        compiler_params=_params(("parallel", "parallel")),
        name="gate_proj",
    )(h, w_gate, b_gate)


def _softmax_pv(q, ks, vs):
    dims = (((1,), (1,)), ((), ()))
    s = [lax.dot_general(q, k, dims, preferred_element_type=F32) for k in ks]
    m = s[0].max(axis=1, keepdims=True)
    for t in s[1:]:
        m = jnp.maximum(m, t.max(axis=1, keepdims=True))
    o = None
    for t, v in zip(s, vs):
        p = jnp.exp2(t - m).astype(BF16)
        part = jnp.dot(p, jnp.concatenate([v, jnp.ones_like(v)], axis=1), preferred_element_type=F32)
        o = part if o is None else o + part
    return o[:, 0:LANES] / o[:, LANES:LANES + 1]


def _diff_lambda(dl, lam_init):
    return (jnp.exp(jnp.sum(dl[0:1, :] * dl[1:2, :], axis=1, keepdims=True))
            - jnp.exp(jnp.sum(dl[2:3, :] * dl[3:4, :], axis=1, keepdims=True)) + lam_init)


def _diff_head(q, ks, vs, lam, g, lam_init):
    qc = q.shape[0]
    qf = q.astype(F32)
    lane = lax.broadcasted_iota(jnp.int32, qf.shape, 1)
    q2 = jnp.concatenate([jnp.where(lane < DIFF_HD, qf, 0.0), jnp.where(lane >= DIFF_HD, qf, 0.0)],
                         axis=0).astype(BF16)
    o = _softmax_pv(q2, ks, vs)
    y = o[0:qc] - lam * o[qc:2 * qc]
    return (_rmsnorm(y, g) * (1.0 - lam_init)).astype(BF16)


def _attn_lat_kernel(dl_ref, g_ref, qd_ref, kdc_ref, kdx_ref, vdc_ref, vdx_ref,
                     qm_ref, kmc_ref, kmx_ref, vmc_ref, vmx_ref, yd_ref, ym_ref, *, lam_init, qc):
    lam = _diff_lambda(dl_ref[...], lam_init)
    kd = (kdc_ref[...], kdx_ref[...])
    vd = (vdc_ref[...], vdx_ref[...])
    km = (kmc_ref[...], kmx_ref[...])
    vm = (vmc_ref[...], vmx_ref[...])
    for c in range(qd_ref.shape[0] // qc):
        rows = slice(c * qc, (c + 1) * qc)
        yd_ref[rows, :] = _diff_head(qd_ref[rows, :], kd, vd, lam, g_ref[...], lam_init)
        ym_ref[rows, :] = _softmax_pv(qm_ref[rows, :], km, vm).astype(BF16)


def _attn_ctx_kernel(dl_ref, g_ref, qkv_ref, qm_ref, km_ref, vm_ref, prev_d, prev_m, yd_ref, ym_ref, *, lam_init):
    del prev_d, prev_m
    lam = _diff_lambda(dl_ref[...], lam_init)
    for h in range(DIFF_HEADS):
        c = slice(h * LANES, (h + 1) * LANES)
        kc = slice(DIFF_W + h * LANES, DIFF_W + (h + 1) * LANES)
        vc = slice(2 * DIFF_W + h * LANES, 2 * DIFF_W + (h + 1) * LANES)
        cm = slice(h * MLA_QK_PAD, (h + 1) * MLA_QK_PAD)
        yd_ref[:, c] = _diff_head(qkv_ref[:, c], (qkv_ref[:, kc],), (qkv_ref[:, vc],), lam, g_ref[...], lam_init)
        ym_ref[:, c] = _softmax_pv(qm_ref[:, cm], (km_ref[:, cm],), (vm_ref[:, c],)).astype(BF16)


def _attn_lat(qkv, qm, km, vm, dl, g, lam_init, out_row0):
    tq = ATTN_TQ
    per_b = SEQ // tq
    row0 = M_CTX // tq
    orow0 = out_row0 // tq
    lat0 = M_CTX // SEQ
    kk, kv = DIFF_HEADS, 2 * DIFF_HEADS
    const = lambda b, h, i: (0, 0)
    q_map = lambda b, h, i: (row0 + b * per_b + i, h)
    o_spec = pl.BlockSpec((tq, LANES), lambda b, h, i: (orow0 + b * per_b + i, h))
    return pl.pallas_call(
        functools.partial(_attn_lat_kernel, lam_init=lam_init, qc=ATTN_QC),
        grid=(BATCH, DIFF_HEADS, per_b),
        in_specs=[
            pl.BlockSpec((4, DIFF_HD), const),
            pl.BlockSpec((1, DIFF_VD), const),
            pl.BlockSpec((tq, LANES), q_map),
            pl.BlockSpec((CTX_LEN, LANES), lambda b, h, i: (b, kk + h)),
            pl.BlockSpec((SEQ, LANES), lambda b, h, i: (lat0 + b, kk + h)),
            pl.BlockSpec((CTX_LEN, LANES), lambda b, h, i: (b, kv + h)),
            pl.BlockSpec((SEQ, LANES), lambda b, h, i: (lat0 + b, kv + h)),
            pl.BlockSpec((tq, MLA_QK_PAD), q_map),
            pl.BlockSpec((CTX_LEN, MLA_QK_PAD), lambda b, h, i: (b, h)),
            pl.BlockSpec((SEQ, MLA_QK_PAD), lambda b, h, i: (lat0 + b, h)),
            pl.BlockSpec((CTX_LEN, LANES), lambda b, h, i: (b, h)),
            pl.BlockSpec((SEQ, LANES), lambda b, h, i: (lat0 + b, h)),
        ],
        out_specs=[o_spec, o_spec],
        out_shape=[jax.ShapeDtypeStruct((out_row0 + M_LAT, DIFF_W), BF16),
                   jax.ShapeDtypeStruct((out_row0 + M_LAT, MLA_W), BF16)],
        compiler_params=_params(("parallel", "parallel", "parallel")),
        name="attn_lat",
    )(dl, g, qkv, qkv, qkv, qkv, qkv, qm, km, km, vm, vm)


def _attn_ctx(qkv, qm, km, vm, dl, g, lam_init, yd, ym):
    const = lambda b: (0, 0)
    rows = lambda b: (b, 0)
    any_spec = pl.BlockSpec(memory_space=pl.ANY)
    return pl.pallas_call(
        functools.partial(_attn_ctx_kernel, lam_init=lam_init),
        grid=(BATCH,),
        in_specs=[
            pl.BlockSpec((4, DIFF_HD), const),
            pl.BlockSpec((1, DIFF_VD), const),
            pl.BlockSpec((CTX_LEN, 3 * DIFF_W), rows),
            pl.BlockSpec((CTX_LEN, MLA_HEADS * MLA_QK_PAD), rows),
            pl.BlockSpec((CTX_LEN, MLA_HEADS * MLA_QK_PAD), rows),
            pl.BlockSpec((CTX_LEN, MLA_W), rows),
            any_spec,
            any_spec,
        ],
        out_specs=[pl.BlockSpec((CTX_LEN, DIFF_W), rows), pl.BlockSpec((CTX_LEN, MLA_W), rows)],
        out_shape=[jax.ShapeDtypeStruct(yd.shape, BF16), jax.ShapeDtypeStruct(ym.shape, BF16)],
        input_output_aliases={6: 0, 7: 1},
        compiler_params=_params(("parallel",)),
        name="attn_ctx",
    )(dl, g, qkv, qm, km, vm, yd, ym)


def _merge_kernel(yd_ref, ym_ref, wd_ref, wm_ref, gd_ref, gm_ref, o_ref):
    yd = jnp.dot(yd_ref[...], wd_ref[...], preferred_element_type=F32)
    ym = jnp.dot(ym_ref[...], wm_ref[...], preferred_element_type=F32)
    o_ref[...] = (gd_ref[...].astype(F32) * yd + gm_ref[...].astype(F32) * ym).astype(BF16)


def _merge(yd, ym, w_od, w_om, gates, row0):
    tm, tn = 1024, 1024
    t0 = row0 // tm
    nj = D_MODEL // tn
    return pl.pallas_call(
        _merge_kernel,
        grid=((M_ALL - row0) // tm, nj),
        in_specs=[
            pl.BlockSpec((tm, DIFF_W), lambda i, j: (i, 0)),
            pl.BlockSpec((tm, MLA_W), lambda i, j: (i, 0)),
            pl.BlockSpec((DIFF_W, tn), lambda i, j: (0, j)),
            pl.BlockSpec((MLA_W, tn), lambda i, j: (0, j)),
            pl.BlockSpec((tm, tn), lambda i, j: (t0 + i, j)),
            pl.BlockSpec((tm, tn), lambda i, j: (t0 + i, nj + j)),
        ],
        out_specs=pl.BlockSpec((tm, tn), lambda i, j: (i, j)),
        out_shape=jax.ShapeDtypeStruct((M_ALL - row0, D_MODEL), BF16),
        compiler_params=_params(("parallel", "parallel")),
        name="merge",
    )(yd, ym, w_od, w_om, gates, gates)


def _out_kernel(a_ref, w_ref, x_ref, g_ref, mod_ref, xo_ref, h_ref):
    mod = mod_ref[...]
    gate1 = mod[:, 2 * D_MODEL:3 * D_MODEL]
    shift2 = mod[:, 3 * D_MODEL:4 * D_MODEL]
    scale2 = mod[:, 4 * D_MODEL:5 * D_MODEL]
    x1 = x_ref[...] + gate1 * jnp.dot(a_ref[...], w_ref[...], preferred_element_type=F32)
    xo_ref[...] = x1
    h_ref[...] = _norm_modulate(x1, g_ref[...], shift2, scale2).astype(BF16)


def _out_proj(merged, w_out, x, g2, mods, row0):
    tm = 512
    t0 = row0 // tm
    tiles_ctx = M_CTX // tm
    tpb = SEQ // tm
    rows = lambda i: (i, 0)
    const = lambda i: (0, 0)
    return pl.pallas_call(
        _out_kernel,
        grid=((M_ALL - row0) // tm,),
        in_specs=[
            pl.BlockSpec((tm, D_MODEL), rows),
            pl.BlockSpec((D_MODEL, D_MODEL), const),
            pl.BlockSpec((tm, D_MODEL), lambda i: (t0 + i, 0)),
            pl.BlockSpec((1, D_MODEL), const),
            pl.BlockSpec((None, 1, N_MOD * D_MODEL), lambda i: (_mod_row(t0 + i, tiles_ctx, tpb), 0, 0)),
        ],
        out_specs=[pl.BlockSpec((tm, D_MODEL), rows), pl.BlockSpec((tm, D_MODEL), rows)],
        out_shape=[
            jax.ShapeDtypeStruct((M_ALL - row0, D_MODEL), F32),
            jax.ShapeDtypeStruct((M_ALL - row0, D_MODEL), BF16),
        ],
        compiler_params=_params(("parallel",)),
        name="out_proj",
    )(merged, w_out, x, g2, mods)


def _mlp_kernel(h_ref, w1_ref, w2_ref, x_ref, mod_ref, gn_ref, modn_ref, *out_refs, final):
    acc_ref = out_refs[-1]
    f = pl.program_id(1)

    @pl.when(f == 0)
    def _():
        acc_ref[...] = jnp.zeros_like(acc_ref)

    u = jnp.maximum(jnp.dot(h_ref[...], w1_ref[...], preferred_element_type=F32), 0.0)
    acc_ref[...] += jnp.dot((u * u).astype(BF16), w2_ref[...], preferred_element_type=F32)

    @pl.when(f == pl.num_programs(1) - 1)
    def _():
        gate2 = mod_ref[...][:, 5 * D_MODEL:6 * D_MODEL]
        x2 = x_ref[...] + gate2 * acc_ref[...]
        if final:
            out_refs[0][...] = _rmsnorm(x2, gn_ref[...])
        else:
            modn = modn_ref[...]
            out_refs[0][...] = x2
            out_refs[1][...] = _norm_modulate(
                x2, gn_ref[...], modn[:, 0:D_MODEL], modn[:, D_MODEL:2 * D_MODEL]).astype(BF16)


def _mlp(h2, w1, w2, x1, mods, g_next, mods_next, row0, final):
    tm, tf = 512, 1024
    t0 = row0 // tm
    tiles_ctx = M_CTX // tm
    tpb = SEQ // tm
    rows = lambda i, f: (i, 0)
    mod_map = lambda i, f: (_mod_row(t0 + i, tiles_ctx, tpb), 0, 0)
    if final:
        out_specs = [pl.BlockSpec((tm, D_MODEL), rows)]
        out_shape = [jax.ShapeDtypeStruct((M_ALL - row0, D_MODEL), F32)]
    else:
        out_specs = [pl.BlockSpec((tm, D_MODEL), rows), pl.BlockSpec((tm, D_MODEL), rows)]
        out_shape = [jax.ShapeDtypeStruct((M_ALL - row0, D_MODEL), F32),
                     jax.ShapeDtypeStruct((M_ALL - row0, D_MODEL), BF16)]
    return pl.pallas_call(
        functools.partial(_mlp_kernel, final=final),
        grid=((M_ALL - row0) // tm, D_FF // tf),
        in_specs=[
            pl.BlockSpec((tm, D_MODEL), rows),
            pl.BlockSpec((D_MODEL, tf), lambda i, f: (0, f)),
            pl.BlockSpec((tf, D_MODEL), lambda i, f: (f, 0)),
            pl.BlockSpec((tm, D_MODEL), rows),
            pl.BlockSpec((None, 1, N_MOD * D_MODEL), mod_map),
            pl.BlockSpec((1, D_MODEL), lambda i, f: (0, 0)),
            pl.BlockSpec((None, 1, N_MOD * D_MODEL), mod_map),
        ],
        out_specs=out_specs,
        out_shape=out_shape,
        scratch_shapes=[pltpu.VMEM((tm, D_MODEL), F32)],
        compiler_params=_params(("parallel", "arbitrary")),
        name="mlp_final" if final else "mlp",
    )(h2, w1, w2, x1, mods, g_next, mods_next)


def _rope_tables():
    rows = SEQ // GRID_W
    row, col = jnp.meshgrid(jnp.arange(rows), jnp.arange(GRID_W), indexing="ij")
    row = row.reshape(-1).astype(F32)
    col = col.reshape(-1).astype(F32)
    freqs = ROPE_BASE ** (-jnp.arange(0, AXIS_DIM, 2, dtype=F32) / AXIS_DIM)
    ang_r = row[:, None] * freqs
    ang_c = col[:, None] * freqs
    ang = jnp.concatenate([ang_r, ang_r, ang_c, ang_c], axis=-1)
    cos, sin = jnp.cos(ang), jnp.sin(ang)
    first = (jnp.arange(ROPE_DIM) & 16) == 0
    sin_signed = jnp.where(first[None, :], -sin, sin)
    ones, zeros = jnp.ones_like(cos), jnp.zeros_like(sin)
    cos2 = jnp.concatenate([cos, cos], axis=-1)
    sin2 = jnp.concatenate([sin_signed, sin_signed], axis=-1)
    blk = jnp.ones((QKV_TM, 2 * ROPE_DIM), F32)
    cos2 = jnp.concatenate([cos2 * LOG2E, cos2, blk * LOG2E, blk], axis=0)
    sin2 = jnp.concatenate([sin2 * LOG2E, sin2, blk * 0.0, blk * 0.0], axis=0)
    cos_pe = jnp.concatenate([cos, ones], axis=-1)
    sin_pe = jnp.concatenate([sin_signed, zeros], axis=-1)
    return cos2, sin2, cos_pe, sin_pe


def kernel(x, c, ctx, c_ctx, w_ada, b_ada, norm1_g, norm2_g, w_in, b_gate, q_a_norm, w_uq, kv_a_norm, w_ukv,
           diff_lambda, diff_subln, w_o_diff, w_o_mla, w_out, w_mlp1, w_mlp2, final_norm_g):
    cos2, sin2, cos_pe, sin_pe = _rope_tables()

    o_q, o_k, o_v, o_cq = 0, DIFF_W, 2 * DIFF_W, 3 * DIFF_W
    o_g = o_cq + MLA_Q_RANK + MLA_KV_RANK + MLA_ROPE
    w_qkv = jnp.concatenate([w_in[:, :, o_q:o_k] * DIFF_SCALE, w_in[:, :, o_k:o_cq]], axis=-1).astype(BF16)
    w_lat = jnp.pad(w_in[:, :, o_cq:o_g], ((0, 0), (0, 0), (0, LAT_W - (o_g - o_cq)))).astype(BF16)
    w_gate = w_in[:, :, o_g:].astype(BF16)
    w_uq_p = jnp.pad(w_uq.reshape(DEPTH, MLA_Q_RANK, MLA_HEADS, MLA_NOPE + MLA_ROPE),
                     ((0, 0), (0, 0), (0, 0), (0, MLA_QK_PAD - MLA_NOPE - MLA_ROPE))
                     ).reshape(DEPTH, MLA_Q_RANK, MLA_HEADS * MLA_QK_PAD).astype(BF16)
    w_ukv4 = w_ukv.reshape(DEPTH, MLA_KV_RANK, MLA_HEADS, MLA_NOPE + MLA_VD)
    w_ukv_p = jnp.concatenate([w_ukv4[..., :MLA_NOPE].reshape(DEPTH, MLA_KV_RANK, MLA_W),
                               w_ukv4[..., MLA_NOPE:].reshape(DEPTH, MLA_KV_RANK, MLA_W)], axis=-1).astype(BF16)
    w_od = w_o_diff.astype(BF16)
    w_om = w_o_mla.astype(BF16)
    w_o = w_out.astype(BF16)
    w1 = w_mlp1.astype(BF16)
    w2 = w_mlp2.astype(BF16)

    cond = jnp.concatenate([c_ctx[None, :], c, jnp.zeros((MOD_ROWS - 1 - BATCH, D_MODEL), F32)], axis=0)
    mods = _ada_table(cond, w_ada, b_ada).reshape(DEPTH, MOD_ROWS, 1, N_MOD * D_MODEL)

    xs, h = _embed(ctx.reshape(M_CTX, D_MODEL), x.reshape(M_LAT, D_MODEL), norm1_g[0:1], mods[0])

    out = None
    for l in range(DEPTH):
        last = l == DEPTH - 1
        lam_init = 0.8 - 0.6 * math.exp(-0.3 * l)
        row0 = M_CTX if last else 0

        qkv = _qkv_proj(h, w_qkv[l], cos2, sin2)
        qm, km, vm = _lat_proj(h, w_lat[l], q_a_norm[l:l + 1], kv_a_norm[l:l + 1], w_uq_p[l], w_ukv_p[l],
                               cos_pe, sin_pe)
        gates = _gate_proj(h, w_gate[l], b_gate[l:l + 1])

        dl, g_sub = diff_lambda[l], diff_subln[l:l + 1]
        yd, ym = _attn_lat(qkv, qm, km, vm, dl, g_sub, lam_init, M_CTX - row0)
        if not last:
            yd, ym = _attn_ctx(qkv, qm, km, vm, dl, g_sub, lam_init, yd, ym)

        merged = _merge(yd, ym, w_od[l], w_om[l], gates, row0)
        x1, h2 = _out_proj(merged, w_o[l], xs, norm2_g[l:l + 1], mods[l], row0)
        if last:
            (out,) = _mlp(h2, w1[l], w2[l], x1, mods[l], final_norm_g[None, :], mods[l], row0, True)
        else:
            xs, h = _mlp(h2, w1[l], w2[l], x1, mods[l], norm1_g[l + 1:l + 2], mods[l + 1], row0, False)
    return out.reshape(BATCH, SEQ, D_MODEL)
```

```python
import functools
import math

import jax
import jax.numpy as jnp
from jax import lax
from jax.experimental import pallas as pl
from jax.experimental.pallas import tpu as pltpu

D_MODEL = 2048
BATCH = 16
SEQ = 2048
DEPTH = 4
CTX_LEN = 256
GRID_W = 64
ROPE_BASE = 10000.0
EPS = 1e-6

DIFF_HEADS = 8
DIFF_HD = 64
DIFF_VD = 2 * DIFF_HD
DIFF_W = DIFF_HEADS * DIFF_VD
DIFF_SCALE = DIFF_HD ** -0.5

MLA_HEADS = 8
MLA_Q_RANK = 512
MLA_KV_RANK = 256
MLA_NOPE = 128
MLA_ROPE = 64
MLA_VD = 128
MLA_W = MLA_HEADS * MLA_VD
MLA_SCALE = (MLA_NOPE + MLA_ROPE) ** -0.5
MLA_QK_PAD = 256

ROPE_DIM = 64
AXIS_DIM = ROPE_DIM // 2
D_FF = 4 * D_MODEL
N_MOD = 6

M_CTX = BATCH * CTX_LEN
M_LAT = BATCH * SEQ
M_ALL = M_CTX + M_LAT
MOD_ROWS = 24

ATTN_TQ, ATTN_QC = 2048, 256
LOG2E = math.log2(math.e)
QKV_TM = 1024

LANES = 128
LAT_W = 1024

F32 = jnp.float32
BF16 = jnp.bfloat16

V7X_VMEM_BYTES = 64 * 1024 * 1024
VMEM_LIMIT = V7X_VMEM_BYTES - 8 * 1024 * 1024


def _params(sem):
    return pltpu.CompilerParams(dimension_semantics=sem, vmem_limit_bytes=VMEM_LIMIT)


def _rmsnorm(x, g):
    return x * lax.rsqrt(jnp.mean(x * x, axis=-1, keepdims=True) + EPS) * g


def _norm_modulate(x, g, shift, scale):
    return _rmsnorm(x, g) * (1.0 + scale) + shift


def _rope128(x, cos, sin_signed):
    lane = lax.broadcasted_iota(jnp.int32, x.shape, 1)
    first = (lane & 16) == 0
    rot = jnp.where(first, pltpu.roll(x, LANES - 16, 1), pltpu.roll(x, 16, 1))
    return x * cos + rot * sin_signed


def _mod_row(i, tiles_ctx, tiles_per_batch):
    return jnp.where(i < tiles_ctx, 0, 1 + (i - tiles_ctx) // tiles_per_batch)


def _ada_kernel(c_ref, w_ref, b_ref, o_ref):
    c = c_ref[...]
    s = (c * jax.nn.sigmoid(c)).astype(BF16)
    o_ref[...] = jnp.dot(s, w_ref[...].astype(BF16), preferred_element_type=F32) + b_ref[...]


def _ada_table(cond, w_ada, b_ada):
    tn = 1024
    n = N_MOD * D_MODEL
    return pl.pallas_call(
        _ada_kernel,
        grid=(DEPTH, n // tn),
        in_specs=[
            pl.BlockSpec((MOD_ROWS, D_MODEL), lambda l, j: (0, 0)),
            pl.BlockSpec((None, D_MODEL, tn), lambda l, j: (l, 0, j)),
            pl.BlockSpec((None, 1, tn), lambda l, j: (l, 0, j)),
        ],
        out_specs=pl.BlockSpec((None, MOD_ROWS, tn), lambda l, j: (l, 0, j)),
        out_shape=jax.ShapeDtypeStruct((DEPTH, MOD_ROWS, n), F32),
        compiler_params=_params(("parallel", "parallel")),
        name="ada_table",
    )(cond, w_ada, b_ada.reshape(DEPTH, 1, n))


def _embed_kernel(ctx_ref, x_ref, g_ref, mod_ref, xo_ref, h_ref, *, tiles_ctx):
    i = pl.program_id(0)
    mod = mod_ref[...]
    shift = mod[:, 0:D_MODEL]
    scale = mod[:, D_MODEL:2 * D_MODEL]

    def emit(src):
        xo_ref[...] = src
        h_ref[...] = _norm_modulate(src, g_ref[...], shift, scale).astype(BF16)

    @pl.when(i < tiles_ctx)
    def _():
        emit(ctx_ref[...])

    @pl.when(i >= tiles_ctx)
    def _():
        emit(x_ref[...])


def _embed(ctx2d, x2d, g, mods):
    tm = 512
    tiles_ctx = M_CTX // tm
    tpb = SEQ // tm
    return pl.pallas_call(
        functools.partial(_embed_kernel, tiles_ctx=tiles_ctx),
        grid=(M_ALL // tm,),
        in_specs=[
            pl.BlockSpec((tm, D_MODEL), lambda i: (jnp.minimum(i, tiles_ctx - 1), 0)),
            pl.BlockSpec((tm, D_MODEL), lambda i: (jnp.maximum(i - tiles_ctx, 0), 0)),
            pl.BlockSpec((1, D_MODEL), lambda i: (0, 0)),
            pl.BlockSpec((None, 1, N_MOD * D_MODEL), lambda i: (_mod_row(i, tiles_ctx, tpb), 0, 0)),
        ],
        out_specs=[
            pl.BlockSpec((tm, D_MODEL), lambda i: (i, 0)),
            pl.BlockSpec((tm, D_MODEL), lambda i: (i, 0)),
        ],
        out_shape=[
            jax.ShapeDtypeStruct((M_ALL, D_MODEL), F32),
            jax.ShapeDtypeStruct((M_ALL, D_MODEL), BF16),
        ],
        compiler_params=_params(("parallel",)),
        name="embed",
    )(ctx2d, x2d, g, mods)


def _qkv_kernel(h_ref, w_ref, cos_ref, sin_ref, o_ref, *, tn, rc):
    for r in range(h_ref.shape[0] // rc):
        rows = slice(r * rc, (r + 1) * rc)
        acc = jnp.dot(h_ref[rows, :], w_ref[...], preferred_element_type=F32)
        cos = cos_ref[rows, :]
        sin = sin_ref[rows, :]
        for c in range(tn // LANES):
            sl = slice(c * LANES, (c + 1) * LANES)
            o_ref[rows, sl] = _rope128(acc[:, sl], cos, sin).astype(BF16)


def _qkv_proj(h, w_qkv, cos2, sin2, l):
    tm, tn = QKV_TM, 1024
    tiles_ctx = M_CTX // tm
    tpb = SEQ // tm

    def table_block(i, j):
        pos = jnp.maximum(i - tiles_ctx, 0) % tpb
        qk = jnp.where(i >= tiles_ctx, j * tpb + pos, 2 * tpb + j)
        return (jnp.where(j == 2, 2 * tpb + 1, qk), 0)

    return pl.pallas_call(
        functools.partial(_qkv_kernel, tn=tn, rc=256),
        grid=(M_ALL // tm, 3 * DIFF_W // tn),
        in_specs=[
            pl.BlockSpec((tm, D_MODEL), lambda i, j: (i, 0)),
            pl.BlockSpec((None, D_MODEL, tn), lambda i, j: (l, 0, j)),
            pl.BlockSpec((tm, LANES), table_block),
            pl.BlockSpec((tm, LANES), table_block),
        ],
        out_specs=pl.BlockSpec((tm, tn), lambda i, j: (i, j)),
        out_shape=jax.ShapeDtypeStruct((M_ALL, 3 * DIFF_W), BF16),
        compiler_params=_params(("parallel", "parallel")),
        name="qkv_proj",
    )(h, w_qkv, cos2, sin2)


def _lat_kernel(h_ref, wlat_ref, qg_ref, kvg_ref, wuq_ref, wukv_ref, cos_ref, sin_ref,
                qm_ref, km_ref, vm_ref, *, tiles_ctx):
    i = pl.program_id(0)
    is_lat = i >= tiles_ctx
    acc = jnp.dot(h_ref[...], wlat_ref[...], preferred_element_type=F32)
    cq = acc[:, 0:MLA_Q_RANK]
    ckv = acc[:, MLA_Q_RANK:MLA_Q_RANK + MLA_KV_RANK]
    kr = acc[:, MLA_Q_RANK + MLA_KV_RANK:MLA_Q_RANK + MLA_KV_RANK + LANES]
    cos = cos_ref[...]
    sin = sin_ref[...]

    cqn = _rmsnorm(cq, qg_ref[...]).astype(BF16)
    qm = jnp.dot(cqn, wuq_ref[...], preferred_element_type=F32) * (MLA_SCALE * LOG2E)
    ckvn = _rmsnorm(ckv, kvg_ref[...]).astype(BF16)
    kv = jnp.dot(ckvn, wukv_ref[...], preferred_element_type=F32)
    kpe = jnp.where(is_lat, _rope128(kr, cos, sin), kr).astype(BF16)

    for hd in range(MLA_HEADS):
        q0 = hd * MLA_QK_PAD
        qm_ref[:, q0:q0 + LANES] = qm[:, q0:q0 + LANES].astype(BF16)
        pe = qm[:, q0 + LANES:q0 + 2 * LANES]
        qm_ref[:, q0 + LANES:q0 + 2 * LANES] = jnp.where(is_lat, _rope128(pe, cos, sin), pe).astype(BF16)
        km_ref[:, q0:q0 + LANES] = kv[:, hd * LANES:(hd + 1) * LANES].astype(BF16)
        km_ref[:, q0 + LANES:q0 + 2 * LANES] = kpe
    vm_ref[...] = kv[:, MLA_W:2 * MLA_W].astype(BF16)


def _lat_proj(h, w_lat, qg, kvg, w_uq, w_ukv, cos_pe, sin_pe, l):
    tm = 512
    tiles_ctx = M_CTX // tm
    tpb = SEQ // tm
    pos = lambda i: (jnp.maximum(i - tiles_ctx, 0) % tpb, 0)
    const = lambda i: (0, 0)
    layer = lambda i: (l, 0, 0)
    qk_w = MLA_HEADS * MLA_QK_PAD
    return pl.pallas_call(
        functools.partial(_lat_kernel, tiles_ctx=tiles_ctx),
        grid=(M_ALL // tm,),
        in_specs=[
            pl.BlockSpec((tm, D_MODEL), lambda i: (i, 0)),
            pl.BlockSpec((None, D_MODEL, LAT_W), layer),
            pl.BlockSpec((1, MLA_Q_RANK), const),
            pl.BlockSpec((1, MLA_KV_RANK), const),
            pl.BlockSpec((None, MLA_Q_RANK, qk_w), layer),
            pl.BlockSpec((None, MLA_KV_RANK, 2 * MLA_W), layer),
            pl.BlockSpec((tm, LANES), pos),
            pl.BlockSpec((tm, LANES), pos),
        ],
        out_specs=[
            pl.BlockSpec((tm, qk_w), lambda i: (i, 0)),
            pl.BlockSpec((tm, qk_w), lambda i: (i, 0)),
            pl.BlockSpec((tm, MLA_W), lambda i: (i, 0)),
        ],
        out_shape=[
            jax.ShapeDtypeStruct((M_ALL, qk_w), BF16),
            jax.ShapeDtypeStruct((M_ALL, qk_w), BF16),
            jax.ShapeDtypeStruct((M_ALL, MLA_W), BF16),
        ],
        compiler_params=_params(("parallel",)),
        name="lat_proj",
    )(h, w_lat, qg, kvg, w_uq, w_ukv, cos_pe, sin_pe)


def _gate_kernel(h_ref, w_ref, b_ref, o_ref):
    rc = 256
    for r in range(h_ref.shape[0] // rc):
        rows = slice(r * rc, (r + 1) * rc)
        acc = jnp.dot(h_ref[rows, :], w_ref[...], preferred_element_type=F32)
        o_ref[rows, :] = jax.nn.sigmoid(acc + b_ref[...]).astype(BF16)


def _gate_proj(h, w_gate, b_gate, l):
    tm, tn = 1024, 1024
    n = 2 * D_MODEL
    return pl.pallas_call(
        _gate_kernel,
        grid=(M_ALL // tm, n // tn),
        in_specs=[
            pl.BlockSpec((tm, D_MODEL), lambda i, j: (i, 0)),
            pl.BlockSpec((None, D_MODEL, tn), lambda i, j: (l, 0, j)),
            pl.BlockSpec((1, tn), lambda i, j: (0, j)),
        ],
        out_specs=pl.BlockSpec((tm, tn), lambda i, j: (i, j)),
        out_shape=jax.ShapeDtypeStruct((M_ALL, n), BF16),
        compiler_params=_params(("parallel", "parallel")),
        name="gate_proj",
    )(h, w_gate, b_gate)


def _softmax_pv(q, ks, vs):
    dims = (((1,), (1,)), ((), ()))
    s = [lax.dot_general(q, k, dims, preferred_element_type=F32) for k in ks]
    m = s[0].max(axis=1, keepdims=True)
    for t in s[1:]:
        m = jnp.maximum(m, t.max(axis=1, keepdims=True))
    o = None
    for t, v in zip(s, vs):
        p = jnp.exp2(t - m).astype(BF16)
        part = jnp.dot(p, jnp.concatenate([v, jnp.ones_like(v)], axis=1), preferred_element_type=F32)
        o = part if o is None else o + part
    return o[:, 0:LANES] / o[:, LANES:LANES + 1]


def _diff_lambda(dl, lam_init):
    return (jnp.exp(jnp.sum(dl[0:1, :] * dl[1:2, :], axis=1, keepdims=True))
            - jnp.exp(jnp.sum(dl[2:3, :] * dl[3:4, :], axis=1, keepdims=True)) + lam_init)


def _diff_head(q, ks, vs, lam, g, lam_init):
    qc = q.shape[0]
    qf = q.astype(F32)
    lane = lax.broadcasted_iota(jnp.int32, qf.shape, 1)
    q2 = jnp.concatenate([jnp.where(lane < DIFF_HD, qf, 0.0), jnp.where(lane >= DIFF_HD, qf, 0.0)],
                         axis=0).astype(BF16)
    o = _softmax_pv(q2, ks, vs)
    y = o[0:qc] - lam * o[qc:2 * qc]
    return (_rmsnorm(y, g) * (1.0 - lam_init)).astype(BF16)


def _attn_lat_kernel(dl_ref, g_ref, qd_ref, kdc_ref, kdx_ref, vdc_ref, vdx_ref,
                     qm_ref, kmc_ref, kmx_ref, vmc_ref, vmx_ref, yd_ref, ym_ref, *, lam_init, qc):
    lam = _diff_lambda(dl_ref[...], lam_init)
    kd = (kdc_ref[...], kdx_ref[...])
    vd = (vdc_ref[...], vdx_ref[...])
    km = (kmc_ref[...], kmx_ref[...])
    vm = (vmc_ref[...], vmx_ref[...])
    for c in range(qd_ref.shape[0] // qc):
        rows = slice(c * qc, (c + 1) * qc)
        yd_ref[rows, :] = _diff_head(qd_ref[rows, :], kd, vd, lam, g_ref[...], lam_init)
        ym_ref[rows, :] = _softmax_pv(qm_ref[rows, :], km, vm).astype(BF16)


def _attn_ctx_kernel(dl_ref, g_ref, qkv_ref, qm_ref, km_ref, vm_ref, prev_d, prev_m, yd_ref, ym_ref, *, lam_init):
    del prev_d, prev_m
    lam = _diff_lambda(dl_ref[...], lam_init)
    for h in range(DIFF_HEADS):
        c = slice(h * LANES, (h + 1) * LANES)
        kc = slice(DIFF_W + h * LANES, DIFF_W + (h + 1) * LANES)
        vc = slice(2 * DIFF_W + h * LANES, 2 * DIFF_W + (h + 1) * LANES)
        cm = slice(h * MLA_QK_PAD, (h + 1) * MLA_QK_PAD)
        yd_ref[:, c] = _diff_head(qkv_ref[:, c], (qkv_ref[:, kc],), (qkv_ref[:, vc],), lam, g_ref[...], lam_init)
        ym_ref[:, c] = _softmax_pv(qm_ref[:, cm], (km_ref[:, cm],), (vm_ref[:, c],)).astype(BF16)


def _attn_lat(qkv, qm, km, vm, dl, g, lam_init, out_row0):
    tq = ATTN_TQ
    per_b = SEQ // tq
    row0 = M_CTX // tq
    orow0 = out_row0 // tq
    lat0 = M_CTX // SEQ
    kk, kv = DIFF_HEADS, 2 * DIFF_HEADS
    const = lambda b, h, i: (0, 0)
    q_map = lambda b, h, i: (row0 + b * per_b + i, h)
    o_spec = pl.BlockSpec((tq, LANES), lambda b, h, i: (orow0 + b * per_b + i, h))
    return pl.pallas_call(
        functools.partial(_attn_lat_kernel, lam_init=lam_init, qc=ATTN_QC),
        grid=(BATCH, DIFF_HEADS, per_b),
        in_specs=[
            pl.BlockSpec((4, DIFF_HD), const),
            pl.BlockSpec((1, DIFF_VD), const),
            pl.BlockSpec((tq, LANES), q_map),
            pl.BlockSpec((CTX_LEN, LANES), lambda b, h, i: (b, kk + h)),
            pl.BlockSpec((SEQ, LANES), lambda b, h, i: (lat0 + b, kk + h)),
            pl.BlockSpec((CTX_LEN, LANES), lambda b, h, i: (b, kv + h)),
            pl.BlockSpec((SEQ, LANES), lambda b, h, i: (lat0 + b, kv + h)),
            pl.BlockSpec((tq, MLA_QK_PAD), q_map),
            pl.BlockSpec((CTX_LEN, MLA_QK_PAD), lambda b, h, i: (b, h)),
            pl.BlockSpec((SEQ, MLA_QK_PAD), lambda b, h, i: (lat0 + b, h)),
            pl.BlockSpec((CTX_LEN, LANES), lambda b, h, i: (b, h)),
            pl.BlockSpec((SEQ, LANES), lambda b, h, i: (lat0 + b, h)),
        ],
        out_specs=[o_spec, o_spec],
        out_shape=[jax.ShapeDtypeStruct((out_row0 + M_LAT, DIFF_W), BF16),
                   jax.ShapeDtypeStruct((out_row0 + M_LAT, MLA_W), BF16)],
        compiler_params=_params(("parallel", "parallel", "parallel")),
        name="attn_lat",
    )(dl, g, qkv, qkv, qkv, qkv, qkv, qm, km, km, vm, vm)


def _attn_ctx(qkv, qm, km, vm, dl, g, lam_init, yd, ym):
    const = lambda b: (0, 0)
    rows = lambda b: (b, 0)
    any_spec = pl.BlockSpec(memory_space=pl.ANY)
    return pl.pallas_call(
        functools.partial(_attn_ctx_kernel, lam_init=lam_init),
        grid=(BATCH,),
        in_specs=[
            pl.BlockSpec((4, DIFF_HD), const),
            pl.BlockSpec((1, DIFF_VD), const),
            pl.BlockSpec((CTX_LEN, 3 * DIFF_W), rows),
            pl.BlockSpec((CTX_LEN, MLA_HEADS * MLA_QK_PAD), rows),
            pl.BlockSpec((CTX_LEN, MLA_HEADS * MLA_QK_PAD), rows),
            pl.BlockSpec((CTX_LEN, MLA_W), rows),
            any_spec,
            any_spec,
        ],
        out_specs=[pl.BlockSpec((CTX_LEN, DIFF_W), rows), pl.BlockSpec((CTX_LEN, MLA_W), rows)],
        out_shape=[jax.ShapeDtypeStruct(yd.shape, BF16), jax.ShapeDtypeStruct(ym.shape, BF16)],
        input_output_aliases={6: 0, 7: 1},
        compiler_params=_params(("parallel",)),
        name="attn_ctx",
    )(dl, g, qkv, qm, km, vm, yd, ym)


def _merge_kernel(yd_ref, ym_ref, wd_ref, wm_ref, gd_ref, gm_ref, o_ref):
    yd = jnp.dot(yd_ref[...], wd_ref[...], preferred_element_type=F32)
    ym = jnp.dot(ym_ref[...], wm_ref[...], preferred_element_type=F32)
    o_ref[...] = (gd_ref[...].astype(F32) * yd + gm_ref[...].astype(F32) * ym).astype(BF16)


def _merge(yd, ym, w_od, w_om, gates, row0, l):
    tm, tn = 1024, 1024
    t0 = row0 // tm
    nj = D_MODEL // tn
    return pl.pallas_call(
        _merge_kernel,
        grid=((M_ALL - row0) // tm, nj),
        in_specs=[
            pl.BlockSpec((tm, DIFF_W), lambda i, j: (i, 0)),
            pl.BlockSpec((tm, MLA_W), lambda i, j: (i, 0)),
            pl.BlockSpec((None, DIFF_W, tn), lambda i, j: (l, 0, j)),
            pl.BlockSpec((None, MLA_W, tn), lambda i, j: (l, 0, j)),
            pl.BlockSpec((tm, tn), lambda i, j: (t0 + i, j)),
            pl.BlockSpec((tm, tn), lambda i, j: (t0 + i, nj + j)),
        ],
        out_specs=pl.BlockSpec((tm, tn), lambda i, j: (i, j)),
        out_shape=jax.ShapeDtypeStruct((M_ALL - row0, D_MODEL), BF16),
        compiler_params=_params(("parallel", "parallel")),
        name="merge",
    )(yd, ym, w_od, w_om, gates, gates)


def _out_kernel(a_ref, w_ref, x_ref, g_ref, mod_ref, xo_ref, h_ref):
    mod = mod_ref[...]
    gate1 = mod[:, 2 * D_MODEL:3 * D_MODEL]
    shift2 = mod[:, 3 * D_MODEL:4 * D_MODEL]
    scale2 = mod[:, 4 * D_MODEL:5 * D_MODEL]
    x1 = x_ref[...] + gate1 * jnp.dot(a_ref[...], w_ref[...], preferred_element_type=F32)
    xo_ref[...] = x1
    h_ref[...] = _norm_modulate(x1, g_ref[...], shift2, scale2).astype(BF16)


def _out_proj(merged, w_out, x, g2, mods, row0, l):
    tm = 512
    t0 = row0 // tm
    tiles_ctx = M_CTX // tm
    tpb = SEQ // tm
    rows = lambda i: (i, 0)
    const = lambda i: (0, 0)
    return pl.pallas_call(
        _out_kernel,
        grid=((M_ALL - row0) // tm,),
        in_specs=[
            pl.BlockSpec((tm, D_MODEL), rows),
            pl.BlockSpec((None, D_MODEL, D_MODEL), lambda i: (l, 0, 0)),
            pl.BlockSpec((tm, D_MODEL), lambda i: (t0 + i, 0)),
            pl.BlockSpec((1, D_MODEL), const),
            pl.BlockSpec((None, 1, N_MOD * D_MODEL), lambda i: (_mod_row(t0 + i, tiles_ctx, tpb), 0, 0)),
        ],
        out_specs=[pl.BlockSpec((tm, D_MODEL), rows), pl.BlockSpec((tm, D_MODEL), rows)],
        out_shape=[
            jax.ShapeDtypeStruct((M_ALL - row0, D_MODEL), F32),
            jax.ShapeDtypeStruct((M_ALL - row0, D_MODEL), BF16),
        ],
        compiler_params=_params(("parallel",)),
        name="out_proj",
    )(merged, w_out, x, g2, mods)


def _mlp_kernel(h_ref, w1_ref, w2_ref, x_ref, mod_ref, gn_ref, modn_ref, *out_refs, final):
    acc_ref = out_refs[-1]
    f = pl.program_id(1)

    @pl.when(f == 0)
    def _():
        acc_ref[...] = jnp.zeros_like(acc_ref)

    u = jnp.maximum(jnp.dot(h_ref[...], w1_ref[...], preferred_element_type=F32), 0.0)
    acc_ref[...] += jnp.dot((u * u).astype(BF16), w2_ref[...], preferred_element_type=F32)

    @pl.when(f == pl.num_programs(1) - 1)
    def _():
        gate2 = mod_ref[...][:, 5 * D_MODEL:6 * D_MODEL]
        x2 = x_ref[...] + gate2 * acc_ref[...]
        if final:
            out_refs[0][...] = _rmsnorm(x2, gn_ref[...])
        else:
            modn = modn_ref[...]
            out_refs[0][...] = x2
            out_refs[1][...] = _norm_modulate(
                x2, gn_ref[...], modn[:, 0:D_MODEL], modn[:, D_MODEL:2 * D_MODEL]).astype(BF16)


def _mlp(h2, w1, w2, x1, mods, g_next, mods_next, row0, final, l):
    tm, tf = 512, 1024
    t0 = row0 // tm
    tiles_ctx = M_CTX // tm
    tpb = SEQ // tm
    rows = lambda i, f: (i, 0)
    mod_map = lambda i, f: (_mod_row(t0 + i, tiles_ctx, tpb), 0, 0)
    if final:
        out_specs = [pl.BlockSpec((tm, D_MODEL), rows)]
        out_shape = [jax.ShapeDtypeStruct((M_ALL - row0, D_MODEL), F32)]
    else:
        out_specs = [pl.BlockSpec((tm, D_MODEL), rows), pl.BlockSpec((tm, D_MODEL), rows)]
        out_shape = [jax.ShapeDtypeStruct((M_ALL - row0, D_MODEL), F32),
                     jax.ShapeDtypeStruct((M_ALL - row0, D_MODEL), BF16)]
    return pl.pallas_call(
        functools.partial(_mlp_kernel, final=final),
        grid=((M_ALL - row0) // tm, D_FF // tf),
        in_specs=[
            pl.BlockSpec((tm, D_MODEL), rows),
            pl.BlockSpec((None, D_MODEL, tf), lambda i, f: (l, 0, f)),
            pl.BlockSpec((None, tf, D_MODEL), lambda i, f: (l, f, 0)),
            pl.BlockSpec((tm, D_MODEL), rows),
            pl.BlockSpec((None, 1, N_MOD * D_MODEL), mod_map),
            pl.BlockSpec((1, D_MODEL), lambda i, f: (0, 0)),
            pl.BlockSpec((None, 1, N_MOD * D_MODEL), mod_map),
        ],
        out_specs=out_specs,
        out_shape=out_shape,
        scratch_shapes=[pltpu.VMEM((tm, D_MODEL), F32)],
        compiler_params=_params(("parallel", "arbitrary")),
        name="mlp_final" if final else "mlp",
    )(h2, w1, w2, x1, mods, g_next, mods_next)


def _rope_tables():
    rows = SEQ // GRID_W
    row, col = jnp.meshgrid(jnp.arange(rows), jnp.arange(GRID_W), indexing="ij")
    row = row.reshape(-1).astype(F32)
    col = col.reshape(-1).astype(F32)
    freqs = ROPE_BASE ** (-jnp.arange(0, AXIS_DIM, 2, dtype=F32) / AXIS_DIM)
    ang_r = row[:, None] * freqs
    ang_c = col[:, None] * freqs
    ang = jnp.concatenate([ang_r, ang_r, ang_c, ang_c], axis=-1)
    cos, sin = jnp.cos(ang), jnp.sin(ang)
    first = (jnp.arange(ROPE_DIM) & 16) == 0
    sin_signed = jnp.where(first[None, :], -sin, sin)
    ones, zeros = jnp.ones_like(cos), jnp.zeros_like(sin)
    cos2 = jnp.concatenate([cos, cos], axis=-1)
    sin2 = jnp.concatenate([sin_signed, sin_signed], axis=-1)
    blk = jnp.ones((QKV_TM, 2 * ROPE_DIM), F32)
    cos2 = jnp.concatenate([cos2 * LOG2E, cos2, blk * LOG2E, blk], axis=0)
    sin2 = jnp.concatenate([sin2 * LOG2E, sin2, blk * 0.0, blk * 0.0], axis=0)
    cos_pe = jnp.concatenate([cos, ones], axis=-1)
    sin_pe = jnp.concatenate([sin_signed, zeros], axis=-1)
    return cos2, sin2, cos_pe, sin_pe


def kernel(x, c, ctx, c_ctx, w_ada, b_ada, norm1_g, norm2_g, w_in, b_gate, q_a_norm, w_uq, kv_a_norm, w_ukv,
           diff_lambda, diff_subln, w_o_diff, w_o_mla, w_out, w_mlp1, w_mlp2, final_norm_g):
    cos2, sin2, cos_pe, sin_pe = _rope_tables()

    o_q, o_k, o_v, o_cq = 0, DIFF_W, 2 * DIFF_W, 3 * DIFF_W
    o_g = o_cq + MLA_Q_RANK + MLA_KV_RANK + MLA_ROPE
    col_scale = jnp.where(jnp.arange(o_cq) < o_k, DIFF_SCALE, 1.0).astype(F32)
    w_qkv = (w_in[:, :, o_q:o_cq] * col_scale).astype(BF16)
    w_lat = jnp.pad(w_in[:, :, o_cq:o_g].astype(BF16), ((0, 0), (0, 0), (0, LAT_W - (o_g - o_cq))))
    w_gate = w_in[:, :, o_g:].astype(BF16)
    w_uq_p = jnp.pad(w_uq.reshape(DEPTH, MLA_Q_RANK, MLA_HEADS, MLA_NOPE + MLA_ROPE),
                     ((0, 0), (0, 0), (0, 0), (0, MLA_QK_PAD - MLA_NOPE - MLA_ROPE))
                     ).reshape(DEPTH, MLA_Q_RANK, MLA_HEADS * MLA_QK_PAD).astype(BF16)
    w_ukv4 = w_ukv.reshape(DEPTH, MLA_KV_RANK, MLA_HEADS, MLA_NOPE + MLA_VD)
    w_ukv_p = jnp.concatenate([w_ukv4[..., :MLA_NOPE].reshape(DEPTH, MLA_KV_RANK, MLA_W),
                               w_ukv4[..., MLA_NOPE:].reshape(DEPTH, MLA_KV_RANK, MLA_W)], axis=-1).astype(BF16)
    w_od = w_o_diff.astype(BF16)
    w_om = w_o_mla.astype(BF16)
    w_o = w_out.astype(BF16)
    w1 = w_mlp1.astype(BF16)
    w2 = w_mlp2.astype(BF16)

    cond = jnp.concatenate([c_ctx[None, :], c, jnp.zeros((MOD_ROWS - 1 - BATCH, D_MODEL), F32)], axis=0)
    mods = _ada_table(cond, w_ada, b_ada).reshape(DEPTH, MOD_ROWS, 1, N_MOD * D_MODEL)

    xs, h = _embed(ctx.reshape(M_CTX, D_MODEL), x.reshape(M_LAT, D_MODEL), norm1_g[0:1], mods[0])

    out = None
    for l in range(DEPTH):
        last = l == DEPTH - 1
        lam_init = 0.8 - 0.6 * math.exp(-0.3 * l)
        row0 = M_CTX if last else 0

        qkv = _qkv_proj(h, w_qkv, cos2, sin2, l)
        qm, km, vm = _lat_proj(h, w_lat, q_a_norm[l:l + 1], kv_a_norm[l:l + 1], w_uq_p, w_ukv_p,
                               cos_pe, sin_pe, l)
        gates = _gate_proj(h, w_gate, b_gate[l:l + 1], l)

        dl, g_sub = diff_lambda[l], diff_subln[l:l + 1]
        yd, ym = _attn_lat(qkv, qm, km, vm, dl, g_sub, lam_init, M_CTX - row0)
        if not last:
            yd, ym = _attn_ctx(qkv, qm, km, vm, dl, g_sub, lam_init, yd, ym)

        merged = _merge(yd, ym, w_od, w_om, gates, row0, l)
        x1, h2 = _out_proj(merged, w_o, xs, norm2_g[l:l + 1], mods[l], row0, l)
        if last:
            (out,) = _mlp(h2, w1, w2, x1, mods[l], final_norm_g[None, :], mods[l], row0, True, l)
        else:
            xs, h = _mlp(h2, w1, w2, x1, mods[l], norm1_g[l + 1:l + 2], mods[l + 1], row0, False, l)
    return out.reshape(BATCH, SEQ, D_MODEL)
```
